```python
import math
import jax, jax.numpy as jnp
from jax import lax
import numpy as np

D_MODEL = 1024
BATCH = 8
SEQ = 2048
DEPTH = 2
DEC_BATCH = 32
DEC_SEQ = 4
PAST_LEN = 16384
PAGE_SIZE = 128

N_MIXERS = 2
N_RET = (DEPTH + 1) // 2
N_ATT = DEPTH // 2
RET_HEADS = 4
RET_DK = D_MODEL // RET_HEADS
RET_DV = 2 * RET_DK
RET_CHUNK = 128
RET_THETA = 10000.0
ATT_HEADS = 16
ATT_DH = D_MODEL // ATT_HEADS
ROT_DIMS = ATT_DH // 4
ROPE_THETA = 500000.0
MOBA_BLOCK = 256
MOBA_TOPK = 3
MOBA_QCHUNK = 16
N_EXPERTS = 32
TOP_K = 4
D_FF = D_MODEL
SWIGLU_LIMIT = 7.0
SWIGLU_ALPHA = 1.702
MOE_BLOCK = 128
DN_ALPHA = (2 * DEPTH) ** 0.25
DN_BETA = (8 * DEPTH) ** -0.25
LN_EPS = 1e-5

kernel_name = 'retnet_moba_moe_hybrid_step'


def layer_norm(x, g, b):
    xf = x.astype(jnp.float32)
    mu = jnp.mean(xf, axis=-1, keepdims=True)
    var = jnp.mean(jnp.square(xf - mu), axis=-1, keepdims=True)
    return ((xf - mu) * lax.rsqrt(var + LN_EPS) * g + b).astype(x.dtype)


def rotary(x, pos, theta, rot_dims):
    half = rot_dims // 2
    inv = jnp.power(theta, -jnp.arange(half, dtype=jnp.float32) / half)
    ang = pos.astype(jnp.float32)[:, None] * inv[None, :]
    cos = jnp.cos(ang)[:, None, :]
    sin = jnp.sin(ang)[:, None, :]
    xf = x.astype(jnp.float32)
    x1 = xf[..., :half]
    x2 = xf[..., half:rot_dims]
    parts = [x1 * cos - x2 * sin, x2 * cos + x1 * sin]
    if rot_dims < x.shape[-1]:
        parts.append(xf[..., rot_dims:])
    return jnp.concatenate(parts, axis=-1).astype(x.dtype)


def ada_modulation(c, w_ada, b_ada):
    m = jax.nn.silu(c) @ w_ada + b_ada
    return jnp.split(m[:, None, :], 6, axis=-1)


def modulate(x, shift, scale):
    return x * (1.0 + scale) + shift


def post_norm_residual(x, f, gate, g, b):
    return layer_norm(DN_ALPHA * x + (1.0 + gate) * f, g, b)


def ret_log_decay():
    return jnp.log1p(-jnp.exp2(-5.0 - jnp.arange(RET_HEADS, dtype=jnp.float32)))


def retention_project(h, w_in, pos):
    b, l, _ = h.shape
    z = h @ w_in
    nq = RET_HEADS * RET_DK
    nv = RET_HEADS * RET_DV
    q = z[..., :nq].reshape(b, l, RET_HEADS, RET_DK)
    k = z[..., nq:2 * nq].reshape(b, l, RET_HEADS, RET_DK)
    v = z[..., 2 * nq:2 * nq + nv].reshape(b, l, RET_HEADS, RET_DV)
    g = z[..., 2 * nq + nv:]
    q = rotary(q, pos, RET_THETA, RET_DK)
    k = rotary(k, pos, RET_THETA, RET_DK) * (RET_DK ** -0.5)
    return q, k, v, g


def retention_block(q, k, v, state, log_decay):
    l = q.shape[1]
    idx = jnp.arange(l, dtype=jnp.float32)
    diff = idx[:, None] - idx[None, :]
    dmat = jnp.where(diff >= 0, jnp.exp(jnp.maximum(diff, 0.0)[None] * log_decay[:, None, None]), 0.0)
    qf, kf, vf = q.astype(jnp.float32), k.astype(jnp.float32), v.astype(jnp.float32)
    scores = jnp.einsum('bihd,bjhd->bhij', qf, kf) * dmat
    inner = jnp.einsum('bhij,bjhe->bihe', scores, vf)
    q_decay = jnp.exp((idx[:, None] + 1.0) * log_decay[None, :])
    cross = jnp.einsum('bihd,bhde->bihe', qf, state) * q_decay[None, :, :, None]
    k_decay = jnp.exp((l - 1.0 - idx)[:, None] * log_decay[None, :])
    new_state = (jnp.exp(l * log_decay)[None, :, None, None] * state
                 + jnp.einsum('bjhd,bjhe->bhde', kf * k_decay[None, :, :, None], vf))
    return inner + cross, new_state


def retention_output(o, g, gn_g, w_out, dtype):
    mu = jnp.mean(o, axis=-1, keepdims=True)
    var = jnp.mean(jnp.square(o - mu), axis=-1, keepdims=True)
    on = (o - mu) * lax.rsqrt(var + LN_EPS) * gn_g
    b, l = o.shape[:2]
    y = jax.nn.silu(g.astype(jnp.float32)) * on.reshape(b, l, RET_HEADS * RET_DV)
    return y.astype(dtype) @ w_out


def retention_prompt(h, w_in, gn_g, w_out):
    b, s, _ = h.shape
    q, k, v, g = retention_project(h, w_in, jnp.arange(s))
    lg = ret_log_decay()
    nc = s // RET_CHUNK

    def to_chunks(a):
        return jnp.moveaxis(a.reshape(b, nc, RET_CHUNK, *a.shape[2:]), 1, 0)

    def step(state, qkv):
        o, state = retention_block(qkv[0], qkv[1], qkv[2], state, lg)
        return state, o

    state0 = jnp.zeros((b, RET_HEADS, RET_DK, RET_DV), jnp.float32)
    state, o = lax.scan(step, state0, (to_chunks(q), to_chunks(k), to_chunks(v)))
    o = jnp.moveaxis(o, 0, 1).reshape(b, s, RET_HEADS, RET_DV)
    return retention_output(o, g, gn_g, w_out, h.dtype), state.astype(h.dtype)


def retention_sample(h, state, w_in, gn_g, w_out):
    n_new = h.shape[1]
    q, k, v, g = retention_project(h, w_in, PAST_LEN + jnp.arange(n_new))
    o, new_state = retention_block(q, k, v, state.astype(jnp.float32), ret_log_decay())
    return retention_output(o, g, gn_g, w_out, h.dtype), new_state.astype(state.dtype)


def attn_project(h, w_qkv, pos):
    b, l, _ = h.shape
    q, k, v = jnp.split(h @ w_qkv, 3, axis=-1)
    shp = (b, l, ATT_HEADS, ATT_DH)
    q = rotary(q.reshape(shp), pos, ROPE_THETA, ROT_DIMS)
    k = rotary(k.reshape(shp), pos, ROPE_THETA, ROT_DIMS)
    return q, k, v.reshape(shp)


def gather_blocks(blocks, idx):
    return jax.vmap(jax.vmap(lambda bl, ix: bl[ix]))(blocks, idx)


def moba_prompt(h, w_qkv, w_out):
    b, s, _ = h.shape
    q, k, v = attn_project(h, w_qkv, jnp.arange(s))
    nb = -(-s // MOBA_BLOCK)
    padn = nb * MOBA_BLOCK - s

    def to_blocks(a):
        a = jnp.pad(a, ((0, 0), (0, padn), (0, 0), (0, 0)))
        return a.reshape(b, nb, MOBA_BLOCK, ATT_HEADS, ATT_DH).transpose(0, 3, 1, 2, 4)

    kb, vb = to_blocks(k), to_blocks(v)
    k_mean = jnp.mean(kb.astype(jnp.float32), axis=3)
    n_sel = min(MOBA_TOPK, nb)
    nq = s // MOBA_QCHUNK
    qc = q.transpose(0, 2, 1, 3).reshape(b, ATT_HEADS, nq, MOBA_QCHUNK, ATT_DH).transpose(2, 0, 1, 3, 4)
    scale = ATT_DH ** -0.5
    blk_ids = jnp.arange(nb)

    def chunk(args):
        qi, ci = args
        t0 = ci * MOBA_QCHUNK
        b_own = t0 // MOBA_BLOCK
        qpos = t0 + jnp.arange(MOBA_QCHUNK)
        qf = qi.astype(jnp.float32)
        gate = jnp.einsum('bhqd,bhnd->bhqn', qf, k_mean)
        gate = jnp.where(blk_ids < b_own, gate, -jnp.inf)
        _, sel = lax.top_k(gate, n_sel)
        valid = sel < b_own
        kg = gather_blocks(kb, sel)
        vg = gather_blocks(vb, sel)
        s_sel = jnp.einsum('bhqd,bhqnkd->bhqnk', qf, kg.astype(jnp.float32)) * scale
        s_sel = jnp.where(valid[..., None], s_sel, -jnp.inf).reshape(b, ATT_HEADS, MOBA_QCHUNK, n_sel * MOBA_BLOCK)
        k_own = lax.dynamic_index_in_dim(kb, b_own, axis=2, keepdims=False)
        v_own = lax.dynamic_index_in_dim(vb, b_own, axis=2, keepdims=False)
        kpos = b_own * MOBA_BLOCK + jnp.arange(MOBA_BLOCK)
        s_own = jnp.einsum('bhqd,bhkd->bhqk', qf, k_own.astype(jnp.float32)) * scale
        s_own = jnp.where(kpos[None, :] <= qpos[:, None], s_own, -jnp.inf)
        p = jax.nn.softmax(jnp.concatenate([s_sel, s_own], axis=-1), axis=-1)
        p_sel = p[..., :n_sel * MOBA_BLOCK].reshape(b, ATT_HEADS, MOBA_QCHUNK, n_sel, MOBA_BLOCK)
        p_own = p[..., n_sel * MOBA_BLOCK:]
        o = (jnp.einsum('bhqnk,bhqnkd->bhqd', p_sel, vg.astype(jnp.float32))
             + jnp.einsum('bhqk,bhkd->bhqd', p_own, v_own.astype(jnp.float32)))
        return o.astype(h.dtype)

    o = lax.map(chunk, (qc, jnp.arange(nq)))
    o = o.transpose(1, 0, 3, 2, 4).reshape(b, s, D_MODEL)
    return o @ w_out, k, v


def moba_sample(h, cache_k, cache_v, page_table, layer, w_qkv, w_out):
    n_dec, n_new, _ = h.shape
    q, k, v = attn_project(h, w_qkv, PAST_LEN + jnp.arange(n_new))
    qf = q.transpose(0, 2, 1, 3).astype(jnp.float32)
    scale = ATT_DH ** -0.5
    ppb = MOBA_BLOCK // PAGE_SIZE
    n_past_blocks = PAST_LEN // MOBA_BLOCK
    own_pages = (PAST_LEN - n_past_blocks * MOBA_BLOCK) // PAGE_SIZE
    own_past = own_pages * PAGE_SIZE
    if own_pages > 0:
        own_phys = page_table[:, n_past_blocks * ppb:]
        k_sh = jnp.concatenate([cache_k[layer, own_phys].reshape(n_dec, own_past, ATT_HEADS, ATT_DH), k], axis=1)
        v_sh = jnp.concatenate([cache_v[layer, own_phys].reshape(n_dec, own_past, ATT_HEADS, ATT_DH), v], axis=1)
    else:
        k_sh, v_sh = k, v
    s_sh = jnp.einsum('bhqd,bkhd->bhqk', qf, k_sh.astype(jnp.float32)) * scale
    kk = jnp.arange(own_past + n_new)
    qq = jnp.arange(n_new)
    s_sh = jnp.where(kk[None, :] <= qq[:, None] + own_past, s_sh, -jnp.inf)
    if n_past_blocks > 0:
        n_sel = min(MOBA_TOPK, n_past_blocks)

        def block_page_sums(pt_row):
            rows = cache_k[layer, pt_row[:n_past_blocks * ppb]].astype(jnp.float32)
            return jnp.sum(rows, axis=1)

        psum = lax.map(block_page_sums, page_table)
        k_mean = psum.reshape(n_dec, n_past_blocks, ppb, ATT_HEADS, ATT_DH).sum(axis=2).transpose(0, 2, 1, 3) / MOBA_BLOCK
        gate = jnp.einsum('bhqd,bhnd->bhqn', qf, k_mean)
        _, sel = lax.top_k(gate, n_sel)
        lpage = sel[..., None] * ppb + jnp.arange(ppb)
        phys = page_table[jnp.arange(n_dec)[:, None, None, None, None], lpage]
        hidx = jnp.arange(ATT_HEADS)[None, :, None, None, None]
        n_rows = n_sel * MOBA_BLOCK
        kg = cache_k[layer, phys, :, hidx].reshape(n_dec, ATT_HEADS, n_new, n_rows, ATT_DH)
        vg = cache_v[layer, phys, :, hidx].reshape(n_dec, ATT_HEADS, n_new, n_rows, ATT_DH)
        s_sel = jnp.einsum('bhqd,bhqkd->bhqk', qf, kg.astype(jnp.float32)) * scale
        p = jax.nn.softmax(jnp.concatenate([s_sel, s_sh], axis=-1), axis=-1)
        o = (jnp.einsum('bhqk,bhqkd->bhqd', p[..., :n_rows], vg.astype(jnp.float32))
             + jnp.einsum('bhqk,bkhd->bhqd', p[..., n_rows:], v_sh.astype(jnp.float32)))
    else:
        p = jax.nn.softmax(s_sh, axis=-1)
        o = jnp.einsum('bhqk,bkhd->bhqd', p, v_sh.astype(jnp.float32))
    o = o.transpose(0, 2, 1, 3).reshape(n_dec, n_new, D_MODEL).astype(h.dtype)
    return o @ w_out, k, v


def moe_ffn(h, w_r, b_r, w_gu, b_gu, w_dn, b_dn):
    shp = h.shape
    t = h.reshape(-1, D_MODEL)
    n_tok = t.shape[0]
    logits = (t @ w_r + b_r).astype(jnp.float32)
    top_val, top_exp = lax.top_k(logits, TOP_K)
    gates = jax.nn.softmax(top_val, axis=-1)
    n_rows = n_tok * TOP_K
    flat_exp = top_exp.reshape(-1)
    flat_tok = jnp.repeat(jnp.arange(n_tok, dtype=jnp.int32), TOP_K)
    flat_gate = gates.reshape(-1)
    order = jnp.argsort(flat_exp)
    sorted_exp = flat_exp[order]
    counts = jnp.bincount(flat_exp, length=N_EXPERTS)
    padded = (counts + MOE_BLOCK - 1) // MOE_BLOCK * MOE_BLOCK
    pad_end = jnp.cumsum(padded)
    pad_start = pad_end - padded
    grp_start = jnp.cumsum(counts) - counts
    dest = pad_start[sorted_exp] + jnp.arange(n_rows) - grp_start[sorted_exp]
    n_blocks = -(-n_rows // MOE_BLOCK) + N_EXPERTS
    n_slots = n_blocks * MOE_BLOCK
    slot_tok = jnp.full((n_slots,), n_tok, jnp.int32).at[dest].set(flat_tok[order])
    slot_gate = jnp.zeros((n_slots,), jnp.float32).at[dest].set(flat_gate[order])
    block_exp = jnp.minimum(jnp.searchsorted(pad_end, jnp.arange(n_blocks) * MOE_BLOCK, side='right'), N_EXPERTS - 1)
    t_ext = jnp.concatenate([t, jnp.zeros((1, D_MODEL), t.dtype)], axis=0)

    def expert_block(args):
        tok, e = args
        xb = t_ext[tok]
        gu = (xb @ w_gu[e] + b_gu[e]).astype(jnp.float32)
        g, u = jnp.split(gu, 2, axis=-1)
        g = jnp.minimum(g, SWIGLU_LIMIT)
        u = jnp.clip(u, -SWIGLU_LIMIT, SWIGLU_LIMIT)
        a = (u + 1.0) * g * jax.nn.sigmoid(SWIGLU_ALPHA * g)
        return a.astype(t.dtype) @ w_dn[e] + b_dn[e]

    out = lax.map(expert_block, (slot_tok.reshape(n_blocks, MOE_BLOCK), block_exp))
    out = out.reshape(n_slots, D_MODEL) * slot_gate[:, None].astype(t.dtype)
    y = jnp.zeros((n_tok + 1, D_MODEL), t.dtype).at[slot_tok].add(out)[:n_tok]
    return y.reshape(shp)


def setup_inputs(seed: int = 0) -> dict:
    key = jax.random.key(seed)
    ks = jax.random.split(key, 26)
    f32 = jnp.float32
    n_pages = PAST_LEN // PAGE_SIZE
    n_pool = (DEC_BATCH * n_pages * 5) // 4

    def nrm(k, shape, scale):
        return jax.random.normal(k, shape, f32) * scale

    ret_cols = 2 * RET_HEADS * RET_DK + 2 * RET_HEADS * RET_DV
    page_table = jax.random.permutation(ks[8], n_pool)[:DEC_BATCH * n_pages].reshape(DEC_BATCH, n_pages).astype(jnp.int32)
    return {
        'x_prompt': nrm(ks[0], (BATCH, SEQ, D_MODEL), 1.0),
        'x_sample': nrm(ks[1], (DEC_BATCH, DEC_SEQ, D_MODEL), 1.0),
        'c_prompt': nrm(ks[2], (BATCH, D_MODEL), 1.0),
        'c_sample': nrm(ks[3], (DEC_BATCH, D_MODEL), 1.0),
        'state_ret': nrm(ks[4], (N_RET, DEC_BATCH, RET_HEADS, RET_DK, RET_DV), 0.5),
        'cache_k': nrm(ks[5], (N_ATT, n_pool, PAGE_SIZE, ATT_HEADS, ATT_DH), 1.0),
        'cache_v': nrm(ks[6], (N_ATT, n_pool, PAGE_SIZE, ATT_HEADS, ATT_DH), 1.0),
        'page_table': page_table,
        'ret_w_in': nrm(ks[9], (N_RET, D_MODEL, ret_cols), D_MODEL ** -0.5),
        'ret_gn_g': 1.0 + nrm(ks[10], (N_RET, RET_HEADS, RET_DV), 0.02),
        'ret_w_out': nrm(ks[11], (N_RET, RET_HEADS * RET_DV, D_MODEL), (RET_HEADS * RET_DV) ** -0.5 * DN_BETA),
        'att_w_qkv': nrm(ks[12], (N_ATT, D_MODEL, 3 * D_MODEL), D_MODEL ** -0.5),
        'att_w_out': nrm(ks[13], (N_ATT, D_MODEL, D_MODEL), D_MODEL ** -0.5 * DN_BETA),
        'ada_w': nrm(ks[14], (DEPTH, D_MODEL, 6 * D_MODEL), 0.1 * D_MODEL ** -0.5),
        'ada_b': nrm(ks[15], (DEPTH, 6 * D_MODEL), 0.01),
        'ln_mix_g': 1.0 + nrm(ks[16], (DEPTH, D_MODEL), 0.02),
        'ln_mix_b': nrm(ks[17], (DEPTH, D_MODEL), 0.02),
        'ln_ffn_g': 1.0 + nrm(ks[18], (DEPTH, D_MODEL), 0.02),
        'ln_ffn_b': nrm(ks[19], (DEPTH, D_MODEL), 0.02),
        'router_w': nrm(ks[20], (DEPTH, D_MODEL, N_EXPERTS), D_MODEL ** -0.5),
        'router_b': nrm(ks[21], (DEPTH, N_EXPERTS), 0.01),
        'exp_w_gu': nrm(ks[22], (DEPTH, N_EXPERTS, D_MODEL, 2 * D_FF), D_MODEL ** -0.5),
        'exp_b_gu': nrm(ks[23], (DEPTH, N_EXPERTS, 2 * D_FF), 0.01),
        'exp_w_down': nrm(ks[24], (DEPTH, N_EXPERTS, D_FF, D_MODEL), D_FF ** -0.5 * DN_BETA),
        'exp_b_down': nrm(ks[25], (DEPTH, N_EXPERTS, D_MODEL), 0.01),
    }


def reference(x_prompt, x_sample, c_prompt, c_sample, state_ret, cache_k, cache_v, page_table,
              ret_w_in, ret_gn_g, ret_w_out, att_w_qkv, att_w_out, ada_w, ada_b,
              ln_mix_g, ln_mix_b, ln_ffn_g, ln_ffn_b, router_w, router_b,
              exp_w_gu, exp_b_gu, exp_w_down, exp_b_down):
    yp, ys = x_prompt, x_sample
    st_p, st_s, kp_l, vp_l, ks_l, vs_l = [], [], [], [], [], []
    for i in range(DEPTH):
        j = i // N_MIXERS
        mp = ada_modulation(c_prompt, ada_w[i], ada_b[i])
        ms = ada_modulation(c_sample, ada_w[i], ada_b[i])
        hp = modulate(yp, mp[0], mp[1])
        hs = modulate(ys, ms[0], ms[1])
        if i % N_MIXERS == 0:
            fp, sp = retention_prompt(hp, ret_w_in[j], ret_gn_g[j], ret_w_out[j])
            fs, ss = retention_sample(hs, state_ret[j], ret_w_in[j], ret_gn_g[j], ret_w_out[j])
            st_p.append(sp)
            st_s.append(ss)
        else:
            fp, kp, vp = moba_prompt(hp, att_w_qkv[j], att_w_out[j])
            fs, kn, vn = moba_sample(hs, cache_k, cache_v, page_table, j, att_w_qkv[j], att_w_out[j])
            kp_l.append(kp)
            vp_l.append(vp)
            ks_l.append(kn)
            vs_l.append(vn)
        yp = post_norm_residual(yp, fp, mp[2], ln_mix_g[i], ln_mix_b[i])
        ys = post_norm_residual(ys, fs, ms[2], ln_mix_g[i], ln_mix_b[i])
        hp = modulate(yp, mp[3], mp[4])
        hs = modulate(ys, ms[3], ms[4])
        fp = moe_ffn(hp, router_w[i], router_b[i], exp_w_gu[i], exp_b_gu[i], exp_w_down[i], exp_b_down[i])
        fs = moe_ffn(hs, router_w[i], router_b[i], exp_w_gu[i], exp_b_gu[i], exp_w_down[i], exp_b_down[i])
        yp = post_norm_residual(yp, fp, mp[5], ln_ffn_g[i], ln_ffn_b[i])
        ys = post_norm_residual(ys, fs, ms[5], ln_ffn_g[i], ln_ffn_b[i])
    return (yp, ys, jnp.stack(st_p), jnp.stack(st_s), jnp.stack(kp_l), jnp.stack(vp_l), jnp.stack(ks_l), jnp.stack(vs_l))
```

```python
import functools

import jax
import jax.numpy as jnp
from jax import lax
from jax.experimental import pallas as pl
from jax.experimental.pallas import tpu as pltpu

F32 = jnp.float32
BF16 = jnp.bfloat16

D = 1024
DEPTH = 2
RET_HEADS, RET_DK, RET_DV, RET_CHUNK, RET_THETA = 4, 256, 512, 128, 10000.0
RET_COLS = 2 * RET_HEADS * RET_DK + 2 * RET_HEADS * RET_DV
ATT_HEADS, ATT_DH, ROT_DIMS, ROPE_THETA = 16, 64, 16, 500000.0
MOBA_BLOCK, MOBA_TOPK, PAGE = 256, 3, 128
N_EXPERTS, TOP_K, D_FF = 32, 4, 1024
SWIGLU_LIMIT, SWIGLU_ALPHA = 7.0, 1.702
DN_ALPHA = (2 * DEPTH) ** 0.25
LN_EPS = 1e-5

TM = 256
SP = 8
TE = 256
LANES = 128
VMEM_LIMIT = 56 * 1024 * 1024
NEG = -1e30


def _cparams(sem, **kw):
    return pltpu.CompilerParams(dimension_semantics=sem, vmem_limit_bytes=VMEM_LIMIT, **kw)


def _bf16_round(x):
    return x.astype(BF16).astype(F32)


def _layer_norm(r, g, b):
    mu = jnp.mean(r, axis=-1, keepdims=True)
    c = r - mu
    var = jnp.mean(c * c, axis=-1, keepdims=True)
    return c * lax.rsqrt(var + LN_EPS) * g + b


def _ada_kernel(c_ref, w_ref, b_ref, o_ref):
    c = c_ref[...]
    s = (c * jax.nn.sigmoid(c)).astype(BF16)
    o_ref[...] = jnp.dot(s, w_ref[...].astype(BF16), preferred_element_type=F32) + b_ref[...]


def ada_mods(c_all, ada_w, ada_b):
    nl, _, ncol = ada_w.shape
    nc = c_all.shape[0]
    tn = 1536
    return pl.pallas_call(
        _ada_kernel,
        grid=(nl, ncol // tn),
        in_specs=[pl.BlockSpec((nc, D), lambda l, j: (0, 0)),
                  pl.BlockSpec((None, D, tn), lambda l, j: (l, 0, j)),
                  pl.BlockSpec((None, 1, tn), lambda l, j: (l, 0, j))],
        out_specs=pl.BlockSpec((None, nc, tn), lambda l, j: (l, 0, j)),
        out_shape=jax.ShapeDtypeStruct((nl, nc, ncol), F32),
        compiler_params=_cparams(("arbitrary", "arbitrary")),
        name="ada_mods",
    )(c_all, ada_w, ada_b.reshape(nl, 1, ncol))


def _ret_proj_kernel(x_ref, sh_ref, sc_ref, w_ref, cos_ref, sin_ref, q_ref, k_ref, v_ref, g_ref):
    h = (x_ref[...] * (1.0 + sc_ref[...]) + sh_ref[...]).astype(BF16)
    cos = cos_ref[...]
    sin = sin_ref[...]
    nq = RET_HEADS * RET_DK
    nv = RET_HEADS * RET_DV
    half = RET_DK // 2
    zq = jnp.dot(h, w_ref[:, 0:nq], preferred_element_type=F32)
    zk = jnp.dot(h, w_ref[:, nq:2 * nq], preferred_element_type=F32)
    for hh in range(RET_HEADS):
        lo = hh * RET_DK
        x1, x2 = zq[:, lo:lo + half], zq[:, lo + half:lo + RET_DK]
        q_ref[:, lo:lo + half] = (x1 * cos - x2 * sin).astype(BF16)
        q_ref[:, lo + half:lo + RET_DK] = (x2 * cos + x1 * sin).astype(BF16)
        y1, y2 = zk[:, lo:lo + half], zk[:, lo + half:lo + RET_DK]
        k_ref[:, lo:lo + half] = (y1 * cos - y2 * sin) * (RET_DK ** -0.5)
        k_ref[:, lo + half:lo + RET_DK] = (y2 * cos + y1 * sin) * (RET_DK ** -0.5)
    v_ref[...] = jnp.dot(h, w_ref[:, 2 * nq:2 * nq + nv], preferred_element_type=F32).astype(BF16)
    g_ref[...] = jnp.dot(h, w_ref[:, 2 * nq + nv:], preferred_element_type=F32)


def ret_proj(x, sh, sc, w_bf, cos_t, sin_t, geo):
    nt, mod_idx, tab_idx = geo["nt"], geo["mod_idx"], geo["tab_idx"]
    t = x.shape[0]
    nq = RET_HEADS * RET_DK
    nv = RET_HEADS * RET_DV
    row = lambda w: pl.BlockSpec((TM, w), lambda i: (i, 0))
    mod = pl.BlockSpec((None, TM, D), lambda i: (mod_idx(i), 0, 0))
    tab = pl.BlockSpec((None, TM, LANES), lambda i: (tab_idx(i), 0, 0))
    return pl.pallas_call(
        _ret_proj_kernel,
        grid=(nt,),
        in_specs=[row(D), mod, mod, pl.BlockSpec((D, RET_COLS), lambda i: (0, 0)), tab, tab],
        out_specs=[row(nq), row(nq), row(nv), row(nv)],
        out_shape=[jax.ShapeDtypeStruct((t, nq), BF16), jax.ShapeDtypeStruct((t, nq), F32),
                   jax.ShapeDtypeStruct((t, nv), BF16), jax.ShapeDtypeStruct((t, nv), F32)],
        compiler_params=_cparams(("arbitrary",)),
        name="ret_proj",
    )(x, sh, sc, w_bf, cos_t, sin_t)


def _ret_chunk_kernel(*refs, has_state):
    if has_state:
        q_ref, k_ref, v_ref, g_ref, dm_ref, qd_ref, kd_ref, sd_ref, gn_ref, st0_ref, y_ref, st_ref = refs
    else:
        q_ref, k_ref, v_ref, g_ref, dm_ref, qd_ref, kd_ref, sd_ref, gn_ref, y_ref, st_ref = refs
    c = pl.program_id(2)

    @pl.when(c == 0)
    def _init():
        if has_state:
            st_ref[...] = st0_ref[...]
        else:
            st_ref[...] = jnp.zeros(st_ref.shape, F32)

    q = q_ref[...].astype(BF16)
    kf = k_ref[...]
    v = v_ref[...].astype(BF16)
    st = st_ref[...]
    scores = lax.dot_general(q, kf.astype(BF16), (((1,), (1,)), ((), ())), preferred_element_type=F32) * dm_ref[...]
    inner = jnp.dot(scores.astype(BF16), v, preferred_element_type=F32)
    cross = jnp.dot(q, st.astype(BF16), preferred_element_type=F32) * qd_ref[...]
    o = inner + cross
    kd = (kf * kd_ref[...]).astype(BF16)
    upd = lax.dot_general(kd, v, (((0,), (0,)), ((), ())), preferred_element_type=F32)
    st_ref[...] = sd_ref[...] * st + upd
    mu = jnp.mean(o, axis=-1, keepdims=True)
    cen = o - mu
    var = jnp.mean(cen * cen, axis=-1, keepdims=True)
    on = cen * lax.rsqrt(var + LN_EPS) * gn_ref[...]
    gg = g_ref[...]
    y_ref[...] = (gg * jax.nn.sigmoid(gg) * on).astype(y_ref.dtype)


def ret_chunks(q, k, v, g, tabs, gn_g, state0, nb, nc, blk_rows, row_blk0, y_dtype):
    dm, qd, kd, sd = tabs
    L = blk_rows
    has_state = state0 is not None
    rb = lambda b, h, c: (row_blk0 + b * nc + c, h)
    head3 = lambda shp: pl.BlockSpec((None,) + shp, lambda b, h, c: (h, 0, 0))
    in_specs = [pl.BlockSpec((L, RET_DK), rb), pl.BlockSpec((L, RET_DK), rb),
                pl.BlockSpec((L, RET_DV), rb), pl.BlockSpec((L, RET_DV), rb),
                head3((L, L)), head3((L, 1)), head3((L, 1)), head3((1, RET_DV)), head3((1, RET_DV))]
    args = [q, k, v, g, dm, qd, kd, sd, gn_g.reshape(RET_HEADS, 1, RET_DV)]
    st_spec = pl.BlockSpec((None, None, RET_DK, RET_DV), lambda b, h, c: (b, h, 0, 0))
    if has_state:
        in_specs.append(st_spec)
        args.append(state0)
    return pl.pallas_call(
        functools.partial(_ret_chunk_kernel, has_state=has_state),
        grid=(nb, RET_HEADS, nc),
        in_specs=in_specs,
        out_specs=[pl.BlockSpec((L, RET_DV), lambda b, h, c: (b * nc + c, h)), st_spec],
        out_shape=[jax.ShapeDtypeStruct((nb * nc * L, RET_HEADS * RET_DV), y_dtype),
                   jax.ShapeDtypeStruct((nb, RET_HEADS, RET_DK, RET_DV), F32)],
        compiler_params=_cparams(("arbitrary", "arbitrary", "arbitrary")),
        name="ret_chunks_state" if has_state else "ret_chunks",
    )(*args)


def _outproj_router_kernel(y_ref, w_ref, x_ref, gate_ref, lng_ref, lnb_ref, sh_ref, sc_ref, wr_ref, br_ref,
                           xn_ref, t_ref, ti_ref, tg_ref):
    f = jnp.dot(y_ref[...], w_ref[...], preferred_element_type=F32)
    r = DN_ALPHA * x_ref[...] + (1.0 + gate_ref[...]) * f
    xn = _layer_norm(r, lng_ref[...], lnb_ref[...])
    xn_ref[...] = xn
    t = xn * (1.0 + sc_ref[...]) + sh_ref[...]
    t_ref[...] = t
    logits = jnp.dot(t.astype(BF16), wr_ref[...], preferred_element_type=F32) + br_ref[...]
    lane = lax.broadcasted_iota(jnp.int32, logits.shape, 1).astype(F32)
    cur = logits
    vals, idxs = [], []
    for _ in range(TOP_K):
        m = jnp.max(cur, axis=1, keepdims=True)
        idx = jnp.min(jnp.where(cur == m, lane, float(LANES)), axis=1, keepdims=True)
        vals.append(m)
        idxs.append(idx)
        cur = jnp.where(lane == idx, -jnp.inf, cur)
    es = [jnp.exp(vv - vals[0]) for vv in vals]
    den = es[0] + es[1] + es[2] + es[3]
    ti = jnp.zeros(logits.shape, F32)
    tg = jnp.zeros(logits.shape, F32)
    for kk in range(TOP_K):
        ti = jnp.where(lane == float(kk), idxs[kk], ti)
        tg = jnp.where(lane == float(kk), es[kk] / den, tg)
    ti_ref[...] = ti
    tg_ref[...] = tg


def outproj_router(y, w_bf, x, gate, lng, lnb, sh, sc, wr_pad, br_pad, geo):
    nt, mod_idx = geo["nt"], geo["mod_idx"]
    t, kdim = y.shape
    row = lambda w: pl.BlockSpec((TM, w), lambda i: (i, 0))
    mod = pl.BlockSpec((None, TM, D), lambda i: (mod_idx(i), 0, 0))
    vec = lambda w: pl.BlockSpec((1, w), lambda i: (0, 0))
    return pl.pallas_call(
        _outproj_router_kernel,
        grid=(nt,),
        in_specs=[row(kdim), pl.BlockSpec((kdim, D), lambda i: (0, 0)), row(D), mod, vec(D), vec(D), mod, mod,
                  pl.BlockSpec((D, LANES), lambda i: (0, 0)), vec(LANES)],
        out_specs=[row(D), row(D), row(LANES), row(LANES)],
        out_shape=[jax.ShapeDtypeStruct((t, D), F32), jax.ShapeDtypeStruct((t, D), F32),
                   jax.ShapeDtypeStruct((t, LANES), F32), jax.ShapeDtypeStruct((t, LANES), F32)],
        compiler_params=_cparams(("arbitrary",)),
        name="outproj_router",
    )(y, w_bf, x, gate, lng.reshape(1, D), lnb.reshape(1, D), sh, sc, wr_pad, br_pad)


def _gather_rows_kernel(nvb_ref, idx_ref, src_ref, dst_ref, sems):
    i = pl.program_id(0)
    n = pl.num_programs(0)
    nvb = nvb_ref[0]

    def row_copy(blk, j, tok):
        return pltpu.make_async_copy(src_ref.at[pl.ds(tok, 1)], dst_ref.at[pl.ds(blk * TE + j, 1)],
                                     sems.at[blk % 2])

    def fill_copy(blk):
        return pltpu.make_async_copy(src_ref.at[pl.ds(0, TE)], dst_ref.at[pl.ds(blk * TE, TE)], sems.at[blk % 2])

    def wait_block(blk):
        @pl.when(blk < nvb)
        def _rows():
            def body(j, c):
                row_copy(blk, j, 0).wait()
                return c
            lax.fori_loop(0, TE, body, 0, unroll=8)

        @pl.when(blk >= nvb)
        def _fill():
            fill_copy(blk).wait()

    @pl.when(i < nvb)
    def _issue():
        def body(j, c):
            row_copy(i, j, idx_ref[0, 0, j]).start()
            return c
        lax.fori_loop(0, TE, body, 0, unroll=8)

    @pl.when(i >= nvb)
    def _issue_fill():
        fill_copy(i).start()

    @pl.when(i >= 1)
    def _wait_prev():
        wait_block(i - 1)

    @pl.when(i == n - 1)
    def _wait_last():
        wait_block(i)


def gather_rows(src, slot_tok, nvb, n_blocks):
    return pl.pallas_call(
        _gather_rows_kernel,
        grid_spec=pltpu.PrefetchScalarGridSpec(
            num_scalar_prefetch=1,
            grid=(n_blocks,),
            in_specs=[pl.BlockSpec((1, 1, TE), lambda i, nv: (i, 0, 0), memory_space=pltpu.SMEM),
                      pl.BlockSpec(memory_space=pl.ANY)],
            out_specs=pl.BlockSpec(memory_space=pl.ANY),
            scratch_shapes=[pltpu.SemaphoreType.DMA((2,))],
        ),
        out_shape=jax.ShapeDtypeStruct((n_blocks * TE, src.shape[1]), src.dtype),
        compiler_params=_cparams(("arbitrary",), has_side_effects=True),
        name="moe_gather_rows",
    )(nvb, slot_tok.reshape(n_blocks, 1, TE), src)


def _expert_kernel(be_ref, nvb_ref, x_ref, wgu_ref, bgu_ref, wdn_ref, bdn_ref, o_ref):
    i = pl.program_id(0)

    @pl.when(i < nvb_ref[0])
    def _compute():
        x = x_ref[...].astype(BF16)
        gu = jnp.dot(x, wgu_ref[...], preferred_element_type=F32) + bgu_ref[...]
        g = jnp.minimum(gu[:, :D_FF], SWIGLU_LIMIT)
        u = jnp.clip(gu[:, D_FF:], -SWIGLU_LIMIT, SWIGLU_LIMIT)
        a = (u + 1.0) * g * jax.nn.sigmoid(SWIGLU_ALPHA * g)
        o_ref[...] = jnp.dot(a.astype(BF16), wdn_ref[...], preferred_element_type=F32) + bdn_ref[...]

    @pl.when(i >= nvb_ref[0])
    def _unused():
        o_ref[...] = jnp.zeros(o_ref.shape, F32)


def expert_blocks(xs, block_exp, nvb, wgu_bf, bgu, wdn_bf, bdn, layer):
    n_blocks = xs.shape[0] // TE
    blk = lambda i, be, nv: (i, 0)
    return pl.pallas_call(
        _expert_kernel,
        grid_spec=pltpu.PrefetchScalarGridSpec(
            num_scalar_prefetch=2,
            grid=(n_blocks,),
            in_specs=[pl.BlockSpec((TE, D), blk),
                      pl.BlockSpec((None, None, D, 2 * D_FF), lambda i, be, nv: (layer, be[i], 0, 0)),
                      pl.BlockSpec((None, None, 1, 2 * D_FF), lambda i, be, nv: (layer, be[i], 0, 0)),
                      pl.BlockSpec((None, None, D_FF, D), lambda i, be, nv: (layer, be[i], 0, 0)),
                      pl.BlockSpec((None, None, 1, D), lambda i, be, nv: (layer, be[i], 0, 0))],
            out_specs=pl.BlockSpec((TE, D), blk),
        ),
        out_shape=jax.ShapeDtypeStruct(xs.shape, F32),
        compiler_params=_cparams(("arbitrary",)),
        name="moe_expert_blocks",
    )(block_exp, nvb, xs, wgu_bf, bgu.reshape(DEPTH, N_EXPERTS, 1, 2 * D_FF), wdn_bf,
      bdn.reshape(DEPTH, N_EXPERTS, 1, D))


def _combine_norm_kernel(idx_ref, nxt_ref, src_ref, tg_ref, x_ref, gate_ref, lng_ref, lnb_ref, o_ref, buf, sems):
    i = pl.program_id(0)
    n = pl.num_programs(0)
    nrow = TOP_K * TM

    def row_copy(tile, j, row):
        return pltpu.make_async_copy(src_ref.at[pl.ds(row, 1)], buf.at[tile % 2, pl.ds(j, 1)], sems.at[tile % 2])

    def issue(tile, ref):
        def body(j, c):
            row_copy(tile, j, ref[0, 0, j]).start()
            return c
        lax.fori_loop(0, nrow, body, 0, unroll=8)

    @pl.when(i == 0)
    def _first():
        issue(i, idx_ref)

    @pl.when(i + 1 < n)
    def _next():
        issue(i + 1, nxt_ref)

    def wbody(j, c):
        row_copy(i, j, 0).wait()
        return c
    lax.fori_loop(0, nrow, wbody, 0, unroll=8)

    slot = i % 2
    tg = tg_ref[...]
    y = jnp.zeros((TM, D), F32)
    for kk in range(TOP_K):
        y = y + buf[slot, pl.ds(kk * TM, TM), :] * tg[:, kk:kk + 1]
    r = DN_ALPHA * x_ref[...] + (1.0 + gate_ref[...]) * y
    o_ref[...] = _layer_norm(r, lng_ref[...], lnb_ref[...])


def combine_norm(out_sorted, dest_tiles, tg, x, gate, lng, lnb, geo):
    nt, mod_idx = geo["nt"], geo["mod_idx"]
    t = x.shape[0]
    row = lambda w: pl.BlockSpec((TM, w), lambda i: (i, 0))
    vec = pl.BlockSpec((1, D), lambda i: (0, 0))
    smem = lambda f: pl.BlockSpec((1, 1, TOP_K * TM), f, memory_space=pltpu.SMEM)
    return pl.pallas_call(
        _combine_norm_kernel,
        grid=(nt,),
        in_specs=[smem(lambda i: (i, 0, 0)), smem(lambda i: (jnp.minimum(i + 1, nt - 1), 0, 0)),
                  pl.BlockSpec(memory_space=pl.ANY), row(LANES), row(D),
                  pl.BlockSpec((None, TM, D), lambda i: (mod_idx(i), 0, 0)), vec, vec],
        out_specs=row(D),
        out_shape=jax.ShapeDtypeStruct((t, D), F32),
        scratch_shapes=[pltpu.VMEM((2, TOP_K * TM, D), F32), pltpu.SemaphoreType.DMA((2,))],
        compiler_params=_cparams(("arbitrary",)),
        name="moe_combine_norm",
    )(dest_tiles, dest_tiles, out_sorted, tg, x, gate, lng.reshape(1, D), lnb.reshape(1, D))


def moe_ffn_norm(t_rows, ti_f, tg, x, gate, lng, lnb, wgu_bf, bgu, wdn_bf, bdn, layer, geo):
    t = t_rows.shape[0]
    nt = geo["nt"]
    ti = ti_f[:, :TOP_K].astype(jnp.int32)
    onehot = jnp.sum((ti[:, :, None] == jnp.arange(N_EXPERTS, dtype=jnp.int32)[None, None, :]).astype(jnp.int32), axis=1)
    csum = jnp.cumsum(onehot, axis=0)
    counts = csum[-1]
    rank = jnp.take_along_axis(csum - onehot, ti, axis=1)
    padded = (counts + TE - 1) // TE * TE
    pad_end = jnp.cumsum(padded)
    pad_start = pad_end - padded
    dest = (pad_start[ti] + rank).astype(jnp.int32)
    n_blocks = (t * TOP_K) // TE + N_EXPERTS
    tok = jnp.broadcast_to(jnp.arange(t, dtype=jnp.int32)[:, None], (t, TOP_K))
    slot_tok = jnp.zeros((n_blocks * TE,), jnp.int32).at[dest.reshape(-1)].set(tok.reshape(-1))
    nvb = (pad_end[-1] // TE).astype(jnp.int32).reshape(1)
    blk_ids = jnp.arange(n_blocks, dtype=jnp.int32)
    block_exp = jnp.minimum(jnp.searchsorted(pad_end, blk_ids * TE, side="right"), N_EXPERTS - 1).astype(jnp.int32)
    block_exp = jnp.where(blk_ids < nvb[0], block_exp, block_exp[nvb[0] - 1])
    xs = gather_rows(t_rows, slot_tok, nvb, n_blocks)
    out_sorted = expert_blocks(xs, block_exp, nvb, wgu_bf, bgu, wdn_bf, bdn, layer)
    dest_tiles = dest.reshape(nt, TM, TOP_K).transpose(0, 2, 1).reshape(nt, 1, TOP_K * TM)
    return combine_norm(out_sorted, dest_tiles, tg, x, gate, lng, lnb, geo)


def _attn_proj_kernel(x_ref, sh_ref, sc_ref, w_ref, c_ref, s1_ref, s2_ref,
                      q_ref, kf_ref, kb_ref, vf_ref, vb_ref, km_ref):
    h = (x_ref[...] * (1.0 + sc_ref[...]) + sh_ref[...]).astype(BF16)
    cc, s1, s2 = c_ref[...], s1_ref[...], s2_ref[...]

    def rot(z):
        return z * cc + pltpu.roll(z, LANES - ROT_DIMS // 2, 1) * s1 + pltpu.roll(z, ROT_DIMS // 2, 1) * s2

    zq = jnp.dot(h, w_ref[:, 0:D], preferred_element_type=F32)
    zk = jnp.dot(h, w_ref[:, D:2 * D], preferred_element_type=F32)
    zv = jnp.dot(h, w_ref[:, 2 * D:3 * D], preferred_element_type=F32)
    for c in range(D // LANES):
        lo = c * LANES
        q_ref[:, lo:lo + LANES] = (rot(zq[:, lo:lo + LANES]) * (ATT_DH ** -0.5)).astype(BF16)
        kr = rot(zk[:, lo:lo + LANES])
        kf_ref[:, lo:lo + LANES] = kr
        kb_ref[:, lo:lo + LANES] = kr.astype(BF16)
        km_ref[:, lo:lo + LANES] = jnp.mean(kr, axis=0, keepdims=True)
    vf_ref[...] = zv
    vb_ref[...] = zv.astype(BF16)


def attn_proj(x, sh, sc, w_bf, ctab, s1tab, s2tab, geo):
    nt, mod_idx, tab_idx = geo["nt"], geo["mod_idx"], geo["tab_idx"]
    t = x.shape[0]
    row = pl.BlockSpec((TM, D), lambda i: (i, 0))
    mod = pl.BlockSpec((None, TM, D), lambda i: (mod_idx(i), 0, 0))
    tab = pl.BlockSpec((None, TM, LANES), lambda i: (tab_idx(i), 0, 0))
    return pl.pallas_call(
        _attn_proj_kernel,
        grid=(nt,),
        in_specs=[row, mod, mod, pl.BlockSpec((D, 3 * D), lambda i: (0, 0)), tab, tab, tab],
        out_specs=[row, row, row, row, row, pl.BlockSpec((None, 1, D), lambda i: (i, 0, 0))],
        out_shape=[jax.ShapeDtypeStruct((t, D), BF16), jax.ShapeDtypeStruct((t, D), F32),
                   jax.ShapeDtypeStruct((t, D), BF16), jax.ShapeDtypeStruct((t, D), F32),
                   jax.ShapeDtypeStruct((t, D), BF16), jax.ShapeDtypeStruct((nt, 1, D), F32)],
        compiler_params=_cparams(("arbitrary",)),
        name="attn_proj",
    )(x, sh, sc, w_bf, ctab, s1tab, s2tab)


def _moba_attn_kernel(q_ref, k_ref, v_ref, km_ref, o_ref, *, nblk):
    qi = pl.program_id(2)
    q = q_ref[...]
    qf = q.astype(F32)
    km = _bf16_round(km_ref[...])
    lane = lax.broadcasted_iota(jnp.int32, (MOBA_BLOCK, LANES), 1)
    rows = lax.broadcasted_iota(jnp.int32, (MOBA_BLOCK, MOBA_BLOCK), 0)
    cols = lax.broadcasted_iota(jnp.int32, (MOBA_BLOCK, MOBA_BLOCK), 1)
    nt_dims = (((1,), (1,)), ((), ()))
    heads = []
    for hh in range(2):
        hm = (lane >= hh * ATT_DH) & (lane < (hh + 1) * ATT_DH)
        qh = jnp.where(hm, q, jnp.zeros_like(q))
        gates = jnp.zeros((MOBA_BLOCK, LANES), F32)
        for nn in range(nblk):
            gn = jnp.sum(jnp.where(hm, qf * km[nn:nn + 1, :], 0.0), axis=1, keepdims=True)
            gates = jnp.where(lane == nn, gn, gates)
        past = lane < qi
        selmask = jnp.zeros((MOBA_BLOCK, LANES), F32)
        for nn in range(nblk - 1):
            gn = gates[:, nn:nn + 1]
            beats = past & ((gates > gn) | ((gates == gn) & (lane < nn)))
            cnt = jnp.sum(jnp.where(beats, 1.0, 0.0), axis=1, keepdims=True)
            selmask = jnp.where((lane == nn) & (cnt < float(MOBA_TOPK)), 1.0, selmask)
        own0 = pl.multiple_of(qi * MOBA_BLOCK, MOBA_BLOCK)
        s = lax.dot_general(qh, k_ref[pl.ds(own0, MOBA_BLOCK), :], nt_dims, preferred_element_type=F32)
        s = jnp.where(cols <= rows, s, -jnp.inf)
        m0 = jnp.max(s, axis=1, keepdims=True)
        p = jnp.exp(s - m0)
        l0 = jnp.sum(p, axis=1, keepdims=True)
        acc0 = jnp.dot(p.astype(BF16), v_ref[pl.ds(own0, MOBA_BLOCK), :], preferred_element_type=F32)
        heads.append((qh, selmask, m0, l0, acc0))

    def body(nn, carry):
        r0 = pl.multiple_of(nn * MOBA_BLOCK, MOBA_BLOCK)
        kblk = k_ref[pl.ds(r0, MOBA_BLOCK), :]
        vblk = v_ref[pl.ds(r0, MOBA_BLOCK), :]
        out = []
        for hh in range(2):
            qh, selmask = heads[hh][0], heads[hh][1]
            m, l, acc = carry[hh]
            flag = jnp.sum(jnp.where(lane == nn, selmask, 0.0), axis=1, keepdims=True)
            s = lax.dot_general(qh, kblk, nt_dims, preferred_element_type=F32)
            s = jnp.where(flag > 0.5, s, -jnp.inf)
            mn = jnp.maximum(m, jnp.max(s, axis=1, keepdims=True))
            a = jnp.exp(m - mn)
            p = jnp.exp(s - mn)
            l = a * l + jnp.sum(p, axis=1, keepdims=True)
            acc = a * acc + jnp.dot(p.astype(BF16), vblk, preferred_element_type=F32)
            out.append((mn, l, acc))
        return tuple(out)

    fin = lax.fori_loop(0, qi, body, tuple((hd[2], hd[3], hd[4]) for hd in heads))
    o0 = fin[0][2] / fin[0][1]
    o1 = fin[1][2] / fin[1][1]
    o_ref[...] = jnp.where(lane < ATT_DH, o0, o1).astype(BF16)


def moba_attn(q, kb, vb, kmean, nb, s_len):
    nblk = s_len // MOBA_BLOCK
    hp = D // LANES
    return pl.pallas_call(
        functools.partial(_moba_attn_kernel, nblk=nblk),
        grid=(nb, hp, nblk),
        in_specs=[pl.BlockSpec((MOBA_BLOCK, LANES), lambda b, h, i: (b * nblk + i, h)),
                  pl.BlockSpec((s_len, LANES), lambda b, h, i: (b, h)),
                  pl.BlockSpec((s_len, LANES), lambda b, h, i: (b, h)),
                  pl.BlockSpec((None, nblk, LANES), lambda b, h, i: (b, 0, h))],
        out_specs=pl.BlockSpec((MOBA_BLOCK, LANES), lambda b, h, i: (b * nblk + i, h)),
        out_shape=jax.ShapeDtypeStruct((nb * s_len, D), BF16),
        compiler_params=_cparams(("arbitrary", "arbitrary", "arbitrary")),
        name="moba_attn",
    )(q, kb, vb, kmean)


def _page_sum_kernel(pt_ref, k_ref, o_ref):
    p = pl.program_id(1)

    @pl.when(p == 0)
    def _zero():
        o_ref[...] = jnp.zeros(o_ref.shape, F32)

    ppb = MOBA_BLOCK // PAGE
    o_ref[p // ppb] += jnp.sum(k_ref[...], axis=0)


def page_sums(cache_k, page_table, layer):
    db, npages = page_table.shape
    nblk = npages * PAGE // MOBA_BLOCK
    return pl.pallas_call(
        _page_sum_kernel,
        grid_spec=pltpu.PrefetchScalarGridSpec(
            num_scalar_prefetch=1,
            grid=(db, npages),
            in_specs=[pl.BlockSpec((None, None, PAGE, ATT_HEADS, ATT_DH), lambda b, p, pt: (layer, pt[b, p], 0, 0, 0))],
            out_specs=pl.BlockSpec((None, nblk, ATT_HEADS, ATT_DH), lambda b, p, pt: (b, 0, 0, 0)),
        ),
        out_shape=jax.ShapeDtypeStruct((db, nblk, ATT_HEADS, ATT_DH), F32),
        compiler_params=_cparams(("arbitrary", "arbitrary")),
        name="moba_page_sums",
    )(page_table, cache_k)


def _sample_select_kernel(ks_ref, q_ref, seg_ref, o_ref, *, n_new):
    km = _bf16_round(ks_ref[...] * (1.0 / MOBA_BLOCK))
    q = q_ref[...]
    seg = seg_ref[...]
    nblk = km.shape[0]
    rowi = lax.broadcasted_iota(jnp.int32, (nblk, LANES), 0).astype(F32)
    orow = lax.broadcasted_iota(jnp.int32, o_ref.shape, 0)
    out = jnp.zeros(o_ref.shape, F32)
    for t in range(n_new):
        prod = km * q[t:t + 1, :]
        p1 = prod.astype(BF16)
        r1 = prod - p1.astype(F32)
        p2 = r1.astype(BF16)
        p3 = (r1 - p2.astype(F32)).astype(BF16)
        g = (jnp.dot(p1, seg, preferred_element_type=F32) + jnp.dot(p2, seg, preferred_element_type=F32)
             + jnp.dot(p3, seg, preferred_element_type=F32))
        for j in range(MOBA_TOPK):
            m = jnp.max(g, axis=0, keepdims=True)
            idx = jnp.min(jnp.where(g == m, rowi, float(nblk)), axis=0, keepdims=True)
            out = jnp.where(orow == t * 4 + j, idx, out)
            g = jnp.where(rowi == idx, -jnp.inf, g)
    o_ref[...] = out


def sample_select(ksum_flat, q, seg, db, n_new, row_blk0):
    nblk = ksum_flat.shape[1]
    return pl.pallas_call(
        functools.partial(_sample_select_kernel, n_new=n_new),
        grid=(db,),
        in_specs=[pl.BlockSpec((None, nblk, D), lambda b: (b, 0, 0)),
                  pl.BlockSpec((SP, D), lambda b: (row_blk0 + b, 0)),
                  pl.BlockSpec((D, LANES), lambda b: (0, 0))],
        out_specs=pl.BlockSpec((None, 16, LANES), lambda b: (b, 0, 0)),
        out_shape=jax.ShapeDtypeStruct((db, 16, LANES), F32),
        compiler_params=_cparams(("arbitrary",)),
        name="moba_sample_select",
    )(ksum_flat, q, seg)


def _sample_attn_kernel(phys_ref, q_ref, kn_ref, vn_ref, ck_ref, cv_ref, o_ref, kbuf, vbuf, sems, *, n_new, layer):
    step = pl.program_id(0) * ATT_HEADS + pl.program_id(1)
    nsteps = pl.num_programs(0) * ATT_HEADS
    ppb = MOBA_BLOCK // PAGE
    per_tok = MOBA_TOPK * MOBA_BLOCK
    per_step = n_new * MOBA_TOPK * ppb

    def copies(st):
        hh = st % ATT_HEADS
        slot = st % 2
        out = []
        for t in range(n_new):
            for j in range(MOBA_TOPK):
                for pp in range(ppb):
                    ph = phys_ref[st * per_step + (t * MOBA_TOPK + j) * ppb + pp]
                    off = t * per_tok + j * MOBA_BLOCK + pp * PAGE
                    out.append(pltpu.make_async_copy(ck_ref.at[layer, ph, :, hh, :],
                                                     kbuf.at[slot, pl.ds(off, PAGE), :], sems.at[slot]))
                    out.append(pltpu.make_async_copy(cv_ref.at[layer, ph, :, hh, :],
                                                     vbuf.at[slot, pl.ds(off, PAGE), :], sems.at[slot]))
        return out

    @pl.when(step == 0)
    def _first():
        for cp in copies(step):
            cp.start()

    @pl.when(step + 1 < nsteps)
    def _next():
        for cp in copies(step + 1):
            cp.start()

    for cp in copies(step):
        cp.wait()

    slot = step % 2
    q = q_ref[...]
    kn = _bf16_round(kn_ref[...])
    vn = _bf16_round(vn_ref[...])
    rowi = lax.broadcasted_iota(jnp.int32, (SP, 1), 0)
    orow = lax.broadcasted_iota(jnp.int32, (SP, ATT_DH), 0)
    out = jnp.zeros((SP, ATT_DH), F32)
    for t in range(n_new):
        qt = q[t:t + 1, :]
        kk = _bf16_round(kbuf[slot, pl.ds(t * per_tok, per_tok), :])
        vv = _bf16_round(vbuf[slot, pl.ds(t * per_tok, per_tok), :])
        s = jnp.sum(kk * qt, axis=1, keepdims=True)
        so = jnp.sum(kn * qt, axis=1, keepdims=True)
        so = jnp.where(rowi <= t, so, -jnp.inf)
        m = jnp.maximum(jnp.max(s, axis=0, keepdims=True), jnp.max(so, axis=0, keepdims=True))
        p = jnp.exp(s - m)
        po = jnp.exp(so - m)
        den = jnp.sum(p, axis=0, keepdims=True) + jnp.sum(po, axis=0, keepdims=True)
        num = (jnp.sum(_bf16_round(p / den) * vv, axis=0, keepdims=True)
               + jnp.sum(_bf16_round(po / den) * vn, axis=0, keepdims=True))
        out = jnp.where(orow == t, num, out)
    o_ref[...] = out


def sample_attn(phys, qh, knh, vnh, cache_k, cache_v, n_new, layer):
    db = qh.shape[0]
    hspec = pl.BlockSpec((None, None, SP, ATT_DH), lambda b, h, ph: (b, h, 0, 0))
    rows = n_new * MOBA_TOPK * MOBA_BLOCK
    return pl.pallas_call(
        functools.partial(_sample_attn_kernel, n_new=n_new, layer=layer),
        grid_spec=pltpu.PrefetchScalarGridSpec(
            num_scalar_prefetch=1,
            grid=(db, ATT_HEADS),
            in_specs=[hspec, hspec, hspec, pl.BlockSpec(memory_space=pl.ANY), pl.BlockSpec(memory_space=pl.ANY)],
            out_specs=hspec,
            scratch_shapes=[pltpu.VMEM((2, rows, ATT_DH), F32), pltpu.VMEM((2, rows, ATT_DH), F32),
                            pltpu.SemaphoreType.DMA((2,))],
        ),
        out_shape=jax.ShapeDtypeStruct((db, ATT_HEADS, SP, ATT_DH), F32),
        compiler_params=_cparams(("arbitrary", "arbitrary")),
        name="moba_sample_attn",
    )(phys, qh, knh, vnh, cache_k, cache_v)


def _rot_tables(pos, theta, rot_dims):
    half = rot_dims // 2
    inv = jnp.power(theta, -jnp.arange(half, dtype=F32) / half)
    ang = pos.astype(F32)[:, None] * inv[None, :]
    return jnp.cos(ang), jnp.sin(ang)


def _positions(s_len, past_len, n_new, n_sample_rows):
    r = jnp.arange(n_sample_rows) % SP
    return jnp.concatenate([jnp.arange(s_len), past_len + jnp.minimum(r, n_new - 1)])


def _attn_rot_tables(pos):
    cos, sin = _rot_tables(pos, ROPE_THETA, ROT_DIMS)
    half = ROT_DIMS // 2
    n = pos.shape[0]
    one = jnp.ones((n, ATT_DH - ROT_DIMS), F32)
    zero8 = jnp.zeros((n, half), F32)
    zrest = jnp.zeros((n, ATT_DH - ROT_DIMS), F32)
    c64 = jnp.concatenate([cos, cos, one], axis=1)
    s1 = jnp.concatenate([-sin, zero8, zrest], axis=1)
    s2 = jnp.concatenate([zero8, sin, zrest], axis=1)
    rep = lambda a: jnp.concatenate([a, a], axis=1).reshape(n // TM, TM, LANES)
    return rep(c64), rep(s1), rep(s2)


def _ret_tables(l_real, l_pad):
    lg = jnp.log1p(-jnp.exp2(-5.0 - jnp.arange(RET_HEADS, dtype=F32)))
    idx = jnp.arange(l_real, dtype=F32)
    diff = idx[:, None] - idx[None, :]
    dmat = jnp.where(diff >= 0, jnp.exp(jnp.maximum(diff, 0.0)[None] * lg[:, None, None]), 0.0)
    q_decay = jnp.exp((idx[:, None] + 1.0) * lg[None, :]).T
    k_decay = jnp.exp((l_real - 1.0 - idx)[:, None] * lg[None, :]).T
    pad = l_pad - l_real
    dmat = jnp.pad(dmat, ((0, 0), (0, pad), (0, pad)))
    q_decay = jnp.pad(q_decay, ((0, 0), (0, pad)))[:, :, None]
    k_decay = jnp.pad(k_decay, ((0, 0), (0, pad)))[:, :, None]
    s_decay = jnp.broadcast_to(jnp.exp(l_real * lg)[:, None, None], (RET_HEADS, 1, RET_DV))
    return dmat, q_decay, k_decay, s_decay


def _mod_tables(m, nb, db):
    out = []
    for j in range(6):
        mj = m[:, j * D:(j + 1) * D]
        mp = jnp.broadcast_to(mj[:nb, None, :], (nb, TM, D))
        ms = jnp.repeat(mj[nb:nb + db], SP, axis=0).reshape(-1, TM, D)
        out.append(jnp.concatenate([mp, ms], axis=0))
    return out


def kernel(x_prompt, x_sample, c_prompt, c_sample, state_ret, cache_k, cache_v, page_table, ret_w_in, ret_gn_g, ret_w_out, att_w_qkv, att_w_out, ada_w, ada_b, ln_mix_g, ln_mix_b, ln_ffn_g, ln_ffn_b, router_w, router_b, exp_w_gu, exp_b_gu, exp_w_down, exp_b_down):
    nb, s_len, _ = x_prompt.shape
    db, n_new, _ = x_sample.shape
    past_len = page_table.shape[1] * PAGE
    assert past_len % MOBA_BLOCK == 0 and s_len % TM == 0 and (db * SP) % TM == 0 and n_new <= SP
    tp, ts = nb * s_len, db * SP
    npt, nst, tpb = tp // TM, ts // TM, s_len // TM
    nt = npt + nst
    geo = {
        "nt": nt,
        "mod_idx": lambda i: jnp.where(i < npt, i // tpb, nb + i - npt),
        "tab_idx": lambda i: jnp.where(i < npt, i % tpb, tpb + i - npt),
    }

    x = jnp.concatenate([x_prompt.reshape(tp, D),
                         jnp.pad(x_sample, ((0, 0), (0, SP - n_new), (0, 0))).reshape(ts, D)], axis=0)
    mods = ada_mods(jnp.concatenate([c_prompt, c_sample], axis=0), ada_w, ada_b)

    pos = _positions(s_len, past_len, n_new, ts)
    rcos, rsin = _rot_tables(pos, RET_THETA, RET_DK)
    rcos, rsin = rcos.reshape(-1, TM, LANES), rsin.reshape(-1, TM, LANES)
    atabs = _attn_rot_tables(pos)
    wgu_bf = exp_w_gu.astype(BF16)
    wdn_bf = exp_w_down.astype(BF16)
    wr_pad = jnp.pad(router_w, ((0, 0), (0, 0), (0, LANES - N_EXPERTS))).astype(BF16)
    br_pad = jnp.pad(router_b, ((0, 0), (0, LANES - N_EXPERTS)), constant_values=NEG)
    seg = (jnp.arange(D)[:, None] // ATT_DH == jnp.arange(LANES)[None, :]).astype(BF16)

    st_p, st_s, kp_l, vp_l, ks_l, vs_l = [], [], [], [], [], []
    for i in range(DEPTH):
        j = i // 2
        m = _mod_tables(mods[i], nb, db)
        if i % 2 == 0:
            q, k, v, g = ret_proj(x, m[0], m[1], ret_w_in[j].astype(BF16), rcos, rsin, geo)
            yp, sp_state = ret_chunks(q, k, v, g, _ret_tables(RET_CHUNK, RET_CHUNK), ret_gn_g[j], None,
                                      nb, s_len // RET_CHUNK, RET_CHUNK, 0, BF16)
            ys, ss_state = ret_chunks(q[tp:].astype(F32), k[tp:], v[tp:].astype(F32), g[tp:], _ret_tables(n_new, SP),
                                      ret_gn_g[j], state_ret[j], db, 1, SP, 0, F32)
            y = jnp.concatenate([yp, ys.astype(BF16)], axis=0)
            w_out = ret_w_out[j].astype(BF16)
            st_p.append(sp_state)
            st_s.append(ss_state)
        else:
            q, kf, kb, vf, vb, kmean = attn_proj(x, m[0], m[1], att_w_qkv[j].astype(BF16), *atabs, geo)
            op = moba_attn(q, kb, vb, kmean[:npt].reshape(nb, tpb, D), nb, s_len)
            ksum = page_sums(cache_k, page_table, j)
            sel = sample_select(ksum.reshape(db, -1, D), q[tp:].astype(F32), seg, db, n_new, 0)
            sel = sel.reshape(db, 4, 4, LANES)[:, :n_new, :MOBA_TOPK, :ATT_HEADS].astype(jnp.int32)
            sel = sel.transpose(0, 3, 1, 2)
            ppb = MOBA_BLOCK // PAGE
            lpage = sel[..., None] * ppb + jnp.arange(ppb, dtype=jnp.int32)
            phys = page_table[jnp.arange(db)[:, None, None, None, None], lpage].reshape(-1).astype(jnp.int32)
            to_heads = lambda a: a[tp:].astype(F32).reshape(db, SP, ATT_HEADS, ATT_DH).transpose(0, 2, 1, 3)
            oh = sample_attn(phys, to_heads(q), to_heads(kf), to_heads(vf), cache_k, cache_v, n_new, j)
            os_rows = oh.transpose(0, 2, 1, 3).reshape(ts, D).astype(BF16)
            y = jnp.concatenate([op, os_rows], axis=0)
            w_out = att_w_out[j].astype(BF16)
            shp_p = (nb, s_len, ATT_HEADS, ATT_DH)
            shp_s = (db, SP, ATT_HEADS, ATT_DH)
            kp_l.append(kf[:tp].reshape(shp_p))
            vp_l.append(vf[:tp].reshape(shp_p))
            ks_l.append(kf[tp:].reshape(shp_s)[:, :n_new])
            vs_l.append(vf[tp:].reshape(shp_s)[:, :n_new])
        x, t_rows, ti_f, tg = outproj_router(y, w_out, x, m[2], ln_mix_g[i], ln_mix_b[i], m[3], m[4],
                                             wr_pad[i], br_pad[i].reshape(1, LANES), geo)
        x = moe_ffn_norm(t_rows, ti_f, tg, x, m[5], ln_ffn_g[i], ln_ffn_b[i],
                         wgu_bf, exp_b_gu, wdn_bf, exp_b_down, i, geo)

    y_prompt = x[:tp].reshape(nb, s_len, D)
    y_sample = x[tp:].reshape(db, SP, D)[:, :n_new]
    return (y_prompt, y_sample, jnp.stack(st_p), jnp.stack(st_s), jnp.stack(kp_l), jnp.stack(vp_l),
            jnp.stack(ks_l), jnp.stack(vs_l))
```

```python
import functools

import jax
import jax.numpy as jnp
from jax import lax
from jax.experimental import pallas as pl
from jax.experimental.pallas import tpu as pltpu

F32 = jnp.float32
BF16 = jnp.bfloat16

D = 1024
DEPTH = 2
RET_HEADS, RET_DK, RET_DV, RET_CHUNK, RET_THETA = 4, 256, 512, 128, 10000.0
RET_COLS = 2 * RET_HEADS * RET_DK + 2 * RET_HEADS * RET_DV
ATT_HEADS, ATT_DH, ROT_DIMS, ROPE_THETA = 16, 64, 16, 500000.0
MOBA_BLOCK, MOBA_TOPK, PAGE = 256, 3, 128
N_EXPERTS, TOP_K, D_FF = 32, 4, 1024
SWIGLU_LIMIT, SWIGLU_ALPHA = 7.0, 1.702
DN_ALPHA = (2 * DEPTH) ** 0.25
LN_EPS = 1e-5

TM = 256
SP = 8
TE = 256
PAGES_PER_STEP = 8
LANES = 128
VMEM_LIMIT = 56 * 1024 * 1024
NEG = -1e30


def _cparams(sem, **kw):
    return pltpu.CompilerParams(dimension_semantics=sem, vmem_limit_bytes=VMEM_LIMIT, **kw)


def _bf16_round(x):
    return x.astype(BF16).astype(F32)


def _layer_norm(r, g, b):
    mu = jnp.mean(r, axis=-1, keepdims=True)
    c = r - mu
    var = jnp.mean(c * c, axis=-1, keepdims=True)
    return c * lax.rsqrt(var + LN_EPS) * g + b


def _ada_kernel(c_ref, w_ref, b_ref, o_ref):
    c = c_ref[...]
    s = (c * jax.nn.sigmoid(c)).astype(BF16)
    o_ref[...] = jnp.dot(s, w_ref[...].astype(BF16), preferred_element_type=F32) + b_ref[...]


def ada_mods(c_all, ada_w, ada_b):
    nl, _, ncol = ada_w.shape
    nc = c_all.shape[0]
    tn = 1536
    return pl.pallas_call(
        _ada_kernel,
        grid=(nl, ncol // tn),
        in_specs=[pl.BlockSpec((nc, D), lambda l, j: (0, 0)),
                  pl.BlockSpec((None, D, tn), lambda l, j: (l, 0, j)),
                  pl.BlockSpec((None, 1, tn), lambda l, j: (l, 0, j))],
        out_specs=pl.BlockSpec((None, nc, tn), lambda l, j: (l, 0, j)),
        out_shape=jax.ShapeDtypeStruct((nl, nc, ncol), F32),
        compiler_params=_cparams(("arbitrary", "arbitrary")),
        name="ada_mods",
    )(c_all, ada_w, ada_b.reshape(nl, 1, ncol))


def _ret_proj_kernel(x_ref, sh_ref, sc_ref, w_ref, cos_ref, sin_ref, q_ref, k_ref, v_ref, g_ref):
    h = (x_ref[...] * (1.0 + sc_ref[...]) + sh_ref[...]).astype(BF16)
    cos = cos_ref[...]
    sin = sin_ref[...]
    nq = RET_HEADS * RET_DK
    nv = RET_HEADS * RET_DV
    half = RET_DK // 2
    zq = jnp.dot(h, w_ref[:, 0:nq], preferred_element_type=F32)
    zk = jnp.dot(h, w_ref[:, nq:2 * nq], preferred_element_type=F32)
    for hh in range(RET_HEADS):
        lo = hh * RET_DK
        x1, x2 = zq[:, lo:lo + half], zq[:, lo + half:lo + RET_DK]
        q_ref[:, lo:lo + half] = (x1 * cos - x2 * sin).astype(BF16)
        q_ref[:, lo + half:lo + RET_DK] = (x2 * cos + x1 * sin).astype(BF16)
        y1, y2 = zk[:, lo:lo + half], zk[:, lo + half:lo + RET_DK]
        k_ref[:, lo:lo + half] = (y1 * cos - y2 * sin) * (RET_DK ** -0.5)
        k_ref[:, lo + half:lo + RET_DK] = (y2 * cos + y1 * sin) * (RET_DK ** -0.5)
    v_ref[...] = jnp.dot(h, w_ref[:, 2 * nq:2 * nq + nv], preferred_element_type=F32).astype(BF16)
    g_ref[...] = jnp.dot(h, w_ref[:, 2 * nq + nv:], preferred_element_type=F32)


def ret_proj(x, sh, sc, w_bf, cos_t, sin_t, geo):
    nt, mod_idx, tab_idx = geo["nt"], geo["mod_idx"], geo["tab_idx"]
    t = x.shape[0]
    nq = RET_HEADS * RET_DK
    nv = RET_HEADS * RET_DV
    row = lambda w: pl.BlockSpec((TM, w), lambda i: (i, 0))
    mod = pl.BlockSpec((None, TM, D), lambda i: (mod_idx(i), 0, 0))
    tab = pl.BlockSpec((None, TM, LANES), lambda i: (tab_idx(i), 0, 0))
    return pl.pallas_call(
        _ret_proj_kernel,
        grid=(nt,),
        in_specs=[row(D), mod, mod, pl.BlockSpec((D, RET_COLS), lambda i: (0, 0)), tab, tab],
        out_specs=[row(nq), row(nq), row(nv), row(nv)],
        out_shape=[jax.ShapeDtypeStruct((t, nq), BF16), jax.ShapeDtypeStruct((t, nq), F32),
                   jax.ShapeDtypeStruct((t, nv), BF16), jax.ShapeDtypeStruct((t, nv), F32)],
        compiler_params=_cparams(("arbitrary",)),
        name="ret_proj",
    )(x, sh, sc, w_bf, cos_t, sin_t)


def _ret_chunk_kernel(*refs, has_state):
    if has_state:
        q_ref, k_ref, v_ref, g_ref, dm_ref, qd_ref, kd_ref, sd_ref, gn_ref, st0_ref, y_ref, st_ref = refs
    else:
        q_ref, k_ref, v_ref, g_ref, dm_ref, qd_ref, kd_ref, sd_ref, gn_ref, y_ref, st_ref = refs
    c = pl.program_id(2)

    @pl.when(c == 0)
    def _init():
        if has_state:
            st_ref[...] = st0_ref[...]
        else:
            st_ref[...] = jnp.zeros(st_ref.shape, F32)

    q = q_ref[...].astype(BF16)
    kf = k_ref[...]
    v = v_ref[...].astype(BF16)
    st = st_ref[...]
    scores = lax.dot_general(q, kf.astype(BF16), (((1,), (1,)), ((), ())), preferred_element_type=F32) * dm_ref[...]
    inner = jnp.dot(scores.astype(BF16), v, preferred_element_type=F32)
    cross = jnp.dot(q, st.astype(BF16), preferred_element_type=F32) * qd_ref[...]
    o = inner + cross
    kd = (kf * kd_ref[...]).astype(BF16)
    upd = lax.dot_general(kd, v, (((0,), (0,)), ((), ())), preferred_element_type=F32)
    st_ref[...] = sd_ref[...] * st + upd
    mu = jnp.mean(o, axis=-1, keepdims=True)
    cen = o - mu
    var = jnp.mean(cen * cen, axis=-1, keepdims=True)
    on = cen * lax.rsqrt(var + LN_EPS) * gn_ref[...]
    gg = g_ref[...]
    y_ref[...] = (gg * jax.nn.sigmoid(gg) * on).astype(y_ref.dtype)


def ret_chunks(q, k, v, g, tabs, gn_g, state0, nb, nc, blk_rows, row_blk0, y_dtype):
    dm, qd, kd, sd = tabs
    L = blk_rows
    has_state = state0 is not None
    rb = lambda b, h, c: (row_blk0 + b * nc + c, h)
    head3 = lambda shp: pl.BlockSpec((None,) + shp, lambda b, h, c: (h, 0, 0))
    in_specs = [pl.BlockSpec((L, RET_DK), rb), pl.BlockSpec((L, RET_DK), rb),
                pl.BlockSpec((L, RET_DV), rb), pl.BlockSpec((L, RET_DV), rb),
                head3((L, L)), head3((L, 1)), head3((L, 1)), head3((1, RET_DV)), head3((1, RET_DV))]
    args = [q, k, v, g, dm, qd, kd, sd, gn_g.reshape(RET_HEADS, 1, RET_DV)]
    st_spec = pl.BlockSpec((None, None, RET_DK, RET_DV), lambda b, h, c: (b, h, 0, 0))
    if has_state:
        in_specs.append(st_spec)
        args.append(state0)
    return pl.pallas_call(
        functools.partial(_ret_chunk_kernel, has_state=has_state),
        grid=(nb, RET_HEADS, nc),
        in_specs=in_specs,
        out_specs=[pl.BlockSpec((L, RET_DV), lambda b, h, c: (b * nc + c, h)), st_spec],
        out_shape=[jax.ShapeDtypeStruct((nb * nc * L, RET_HEADS * RET_DV), y_dtype),
                   jax.ShapeDtypeStruct((nb, RET_HEADS, RET_DK, RET_DV), F32)],
        compiler_params=_cparams(("arbitrary", "arbitrary", "arbitrary")),
        name="ret_chunks_state" if has_state else "ret_chunks",
    )(*args)


def _outproj_router_kernel(y_ref, w_ref, x_ref, gate_ref, lng_ref, lnb_ref, sh_ref, sc_ref, wr_ref, br_ref,
                           xn_ref, t_ref, ti_ref, tg_ref):
    f = jnp.dot(y_ref[...], w_ref[...], preferred_element_type=F32)
    r = DN_ALPHA * x_ref[...] + (1.0 + gate_ref[...]) * f
    xn = _layer_norm(r, lng_ref[...], lnb_ref[...])
    xn_ref[...] = xn
    t = xn * (1.0 + sc_ref[...]) + sh_ref[...]
    t_ref[...] = t
    logits = jnp.dot(t.astype(BF16), wr_ref[...], preferred_element_type=F32) + br_ref[...]
    lane = lax.broadcasted_iota(jnp.int32, logits.shape, 1).astype(F32)
    cur = logits
    vals, idxs = [], []
    for _ in range(TOP_K):
        m = jnp.max(cur, axis=1, keepdims=True)
        idx = jnp.min(jnp.where(cur == m, lane, float(LANES)), axis=1, keepdims=True)
        vals.append(m)
        idxs.append(idx)
        cur = jnp.where(lane == idx, -jnp.inf, cur)
    es = [jnp.exp(vv - vals[0]) for vv in vals]
    den = es[0] + es[1] + es[2] + es[3]
    ti = jnp.zeros(logits.shape, F32)
    tg = jnp.zeros(logits.shape, F32)
    for kk in range(TOP_K):
        ti = jnp.where(lane == float(kk), idxs[kk], ti)
        tg = jnp.where(lane == float(kk), es[kk] / den, tg)
    ti_ref[...] = ti
    tg_ref[...] = tg


def outproj_router(y, w_bf, x, gate, lng, lnb, sh, sc, wr_pad, br_pad, geo):
    nt, mod_idx = geo["nt"], geo["mod_idx"]
    t, kdim = y.shape
    row = lambda w: pl.BlockSpec((TM, w), lambda i: (i, 0))
    mod = pl.BlockSpec((None, TM, D), lambda i: (mod_idx(i), 0, 0))
    vec = lambda w: pl.BlockSpec((1, w), lambda i: (0, 0))
    return pl.pallas_call(
        _outproj_router_kernel,
        grid=(nt,),
        in_specs=[row(kdim), pl.BlockSpec((kdim, D), lambda i: (0, 0)), row(D), mod, vec(D), vec(D), mod, mod,
                  pl.BlockSpec((D, LANES), lambda i: (0, 0)), vec(LANES)],
        out_specs=[row(D), row(D), row(LANES), row(LANES)],
        out_shape=[jax.ShapeDtypeStruct((t, D), F32), jax.ShapeDtypeStruct((t, D), F32),
                   jax.ShapeDtypeStruct((t, LANES), F32), jax.ShapeDtypeStruct((t, LANES), F32)],
        compiler_params=_cparams(("arbitrary",)),
        name="outproj_router",
    )(y, w_bf, x, gate, lng.reshape(1, D), lnb.reshape(1, D), sh, sc, wr_pad, br_pad)


def _expert_kernel(be_ref, nvb_ref, idx_ref, nxt_ref, src_ref, wgu_ref, bgu_ref, wdn_ref, bdn_ref, o_ref,
                   xbuf, wgu_bf, wdn_bf, sems):
    i = pl.program_id(0)
    nvb = nvb_ref[0]

    def row_copy(blk, j, tok):
        return pltpu.make_async_copy(src_ref.at[pl.ds(tok, 1)], xbuf.at[blk % 2, pl.ds(j, 1)], sems.at[blk % 2])

    def issue(blk, ref):
        def body(j, c):
            row_copy(blk, j, ref[0, 0, j]).start()
            return c
        lax.fori_loop(0, TE, body, 0, unroll=8)

    @pl.when(i == 0)
    def _first():
        issue(i, idx_ref)

    @pl.when(i + 1 < nvb)
    def _next():
        issue(i + 1, nxt_ref)

    new_expert = (i == 0) | (be_ref[i] != be_ref[jnp.maximum(i - 1, 0)])

    @pl.when((i < nvb) & new_expert)
    def _cast_weights():
        wgu_bf[...] = wgu_ref[...].astype(BF16)
        wdn_bf[...] = wdn_ref[...].astype(BF16)

    @pl.when(i < nvb)
    def _compute():
        def wbody(j, c):
            row_copy(i, j, 0).wait()
            return c
        lax.fori_loop(0, TE, wbody, 0, unroll=8)
        x = xbuf[i % 2].astype(BF16)
        gu = jnp.dot(x, wgu_bf[...], preferred_element_type=F32) + bgu_ref[...]
        g = jnp.minimum(gu[:, :D_FF], SWIGLU_LIMIT)
        u = jnp.clip(gu[:, D_FF:], -SWIGLU_LIMIT, SWIGLU_LIMIT)
        a = (u + 1.0) * g * jax.nn.sigmoid(SWIGLU_ALPHA * g)
        o_ref[...] = jnp.dot(a.astype(BF16), wdn_bf[...], preferred_element_type=F32) + bdn_ref[...]

    @pl.when(i >= nvb)
    def _unused():
        o_ref[...] = jnp.zeros(o_ref.shape, F32)


def expert_blocks(t_rows, slot_tok, block_exp, nvb, wgu, bgu, wdn, bdn, layer, n_blocks):
    smem = lambda f: pl.BlockSpec((1, 1, TE), f, memory_space=pltpu.SMEM)
    wspec = lambda r, c: pl.BlockSpec((None, None, r, c), lambda i, be, nv: (layer, be[i], 0, 0))
    return pl.pallas_call(
        _expert_kernel,
        grid_spec=pltpu.PrefetchScalarGridSpec(
            num_scalar_prefetch=2,
            grid=(n_blocks,),
            in_specs=[smem(lambda i, be, nv: (i, 0, 0)),
                      smem(lambda i, be, nv: (jnp.minimum(i + 1, n_blocks - 1), 0, 0)),
                      pl.BlockSpec(memory_space=pl.ANY),
                      wspec(D, 2 * D_FF), wspec(1, 2 * D_FF), wspec(D_FF, D), wspec(1, D)],
            out_specs=pl.BlockSpec((TE, D), lambda i, be, nv: (i, 0)),
            scratch_shapes=[pltpu.VMEM((2, TE, D), F32), pltpu.VMEM((D, 2 * D_FF), BF16),
                            pltpu.VMEM((D_FF, D), BF16), pltpu.SemaphoreType.DMA((2,))],
        ),
        out_shape=jax.ShapeDtypeStruct((n_blocks * TE, D), F32),
        compiler_params=_cparams(("arbitrary",)),
        name="moe_expert_blocks",
    )(block_exp, nvb, slot_tok.reshape(n_blocks, 1, TE), slot_tok.reshape(n_blocks, 1, TE), t_rows,
      wgu, bgu.reshape(DEPTH, N_EXPERTS, 1, 2 * D_FF), wdn, bdn.reshape(DEPTH, N_EXPERTS, 1, D))


def _combine_norm_kernel(idx_ref, nxt_ref, src_ref, tg_ref, x_ref, gate_ref, lng_ref, lnb_ref, o_ref, buf, sems):
    i = pl.program_id(0)
    n = pl.num_programs(0)
    nrow = TOP_K * TM

    def row_copy(tile, j, row):
        return pltpu.make_async_copy(src_ref.at[pl.ds(row, 1)], buf.at[tile % 2, pl.ds(j, 1)], sems.at[tile % 2])

    def issue(tile, ref):
        def body(j, c):
            row_copy(tile, j, ref[0, 0, j]).start()
            return c
        lax.fori_loop(0, nrow, body, 0, unroll=8)

    @pl.when(i == 0)
    def _first():
        issue(i, idx_ref)

    @pl.when(i + 1 < n)
    def _next():
        issue(i + 1, nxt_ref)

    def wbody(j, c):
        row_copy(i, j, 0).wait()
        return c
    lax.fori_loop(0, nrow, wbody, 0, unroll=8)

    slot = i % 2
    tg = tg_ref[...]
    y = jnp.zeros((TM, D), F32)
    for kk in range(TOP_K):
        y = y + buf[slot, pl.ds(kk * TM, TM), :] * tg[:, kk:kk + 1]
    r = DN_ALPHA * x_ref[...] + (1.0 + gate_ref[...]) * y
    o_ref[...] = _layer_norm(r, lng_ref[...], lnb_ref[...])


def combine_norm(out_sorted, dest_tiles, tg, x, gate, lng, lnb, geo):
    nt, mod_idx = geo["nt"], geo["mod_idx"]
    t = x.shape[0]
    row = lambda w: pl.BlockSpec((TM, w), lambda i: (i, 0))
    vec = pl.BlockSpec((1, D), lambda i: (0, 0))
    smem = lambda f: pl.BlockSpec((1, 1, TOP_K * TM), f, memory_space=pltpu.SMEM)
    return pl.pallas_call(
        _combine_norm_kernel,
        grid=(nt,),
        in_specs=[smem(lambda i: (i, 0, 0)), smem(lambda i: (jnp.minimum(i + 1, nt - 1), 0, 0)),
                  pl.BlockSpec(memory_space=pl.ANY), row(LANES), row(D),
                  pl.BlockSpec((None, TM, D), lambda i: (mod_idx(i), 0, 0)), vec, vec],
        out_specs=row(D),
        out_shape=jax.ShapeDtypeStruct((t, D), F32),
        scratch_shapes=[pltpu.VMEM((2, TOP_K * TM, D), F32), pltpu.SemaphoreType.DMA((2,))],
        compiler_params=_cparams(("arbitrary",)),
        name="moe_combine_norm",
    )(dest_tiles, dest_tiles, out_sorted, tg, x, gate, lng.reshape(1, D), lnb.reshape(1, D))


def moe_ffn_norm(t_rows, ti_f, tg, x, gate, lng, lnb, wgu, bgu, wdn, bdn, layer, geo):
    t = t_rows.shape[0]
    nt = geo["nt"]
    ti = ti_f[:, :TOP_K].astype(jnp.int32)
    onehot = jnp.sum((ti[:, :, None] == jnp.arange(N_EXPERTS, dtype=jnp.int32)[None, None, :]).astype(jnp.int32), axis=1)
    csum = jnp.cumsum(onehot, axis=0)
    counts = csum[-1]
    rank = jnp.take_along_axis(csum - onehot, ti, axis=1)
    padded = (counts + TE - 1) // TE * TE
    pad_end = jnp.cumsum(padded)
    pad_start = pad_end - padded
    dest = (pad_start[ti] + rank).astype(jnp.int32)
    n_blocks = (t * TOP_K) // TE + N_EXPERTS
    tok = jnp.broadcast_to(jnp.arange(t, dtype=jnp.int32)[:, None], (t, TOP_K))
    slot_tok = jnp.zeros((n_blocks * TE,), jnp.int32).at[dest.reshape(-1)].set(tok.reshape(-1))
    nvb = (pad_end[-1] // TE).astype(jnp.int32).reshape(1)
    blk_ids = jnp.arange(n_blocks, dtype=jnp.int32)
    block_exp = jnp.minimum(jnp.searchsorted(pad_end, blk_ids * TE, side="right"), N_EXPERTS - 1).astype(jnp.int32)
    block_exp = jnp.where(blk_ids < nvb[0], block_exp, block_exp[nvb[0] - 1])
    out_sorted = expert_blocks(t_rows, slot_tok, block_exp, nvb, wgu, bgu, wdn, bdn, layer, n_blocks)
    dest_tiles = dest.reshape(nt, TM, TOP_K).transpose(0, 2, 1).reshape(nt, 1, TOP_K * TM)
    return combine_norm(out_sorted, dest_tiles, tg, x, gate, lng, lnb, geo)


def _attn_proj_kernel(x_ref, sh_ref, sc_ref, w_ref, c_ref, s1_ref, s2_ref,
                      q_ref, kf_ref, kb_ref, vf_ref, vb_ref, km_ref):
    h = (x_ref[...] * (1.0 + sc_ref[...]) + sh_ref[...]).astype(BF16)
    cc, s1, s2 = c_ref[...], s1_ref[...], s2_ref[...]

    def rot(z):
        return z * cc + pltpu.roll(z, LANES - ROT_DIMS // 2, 1) * s1 + pltpu.roll(z, ROT_DIMS // 2, 1) * s2

    zq = jnp.dot(h, w_ref[:, 0:D], preferred_element_type=F32)
    zk = jnp.dot(h, w_ref[:, D:2 * D], preferred_element_type=F32)
    zv = jnp.dot(h, w_ref[:, 2 * D:3 * D], preferred_element_type=F32)
    for c in range(D // LANES):
        lo = c * LANES
        q_ref[:, lo:lo + LANES] = (rot(zq[:, lo:lo + LANES]) * (ATT_DH ** -0.5)).astype(BF16)
        kr = rot(zk[:, lo:lo + LANES])
        kf_ref[:, lo:lo + LANES] = kr
        kb_ref[:, lo:lo + LANES] = kr.astype(BF16)
        km_ref[:, lo:lo + LANES] = jnp.mean(kr, axis=0, keepdims=True)
    vf_ref[...] = zv
    vb_ref[...] = zv.astype(BF16)


def attn_proj(x, sh, sc, w_bf, ctab, s1tab, s2tab, geo):
    nt, mod_idx, tab_idx = geo["nt"], geo["mod_idx"], geo["tab_idx"]
    t = x.shape[0]
    row = pl.BlockSpec((TM, D), lambda i: (i, 0))
    mod = pl.BlockSpec((None, TM, D), lambda i: (mod_idx(i), 0, 0))
    tab = pl.BlockSpec((None, TM, LANES), lambda i: (tab_idx(i), 0, 0))
    return pl.pallas_call(
        _attn_proj_kernel,
        grid=(nt,),
        in_specs=[row, mod, mod, pl.BlockSpec((D, 3 * D), lambda i: (0, 0)), tab, tab, tab],
        out_specs=[row, row, row, row, row, pl.BlockSpec((None, 1, D), lambda i: (i, 0, 0))],
        out_shape=[jax.ShapeDtypeStruct((t, D), BF16), jax.ShapeDtypeStruct((t, D), F32),
                   jax.ShapeDtypeStruct((t, D), BF16), jax.ShapeDtypeStruct((t, D), F32),
                   jax.ShapeDtypeStruct((t, D), BF16), jax.ShapeDtypeStruct((nt, 1, D), F32)],
        compiler_params=_cparams(("arbitrary",)),
        name="attn_proj",
    )(x, sh, sc, w_bf, ctab, s1tab, s2tab)


def _moba_attn_kernel(q_ref, k_ref, v_ref, km_ref, o_ref):
    qi = pl.program_id(2)
    q = q_ref[...]
    km = km_ref[...].astype(BF16)
    lane = lax.broadcasted_iota(jnp.int32, (MOBA_BLOCK, LANES), 1)
    lanef = lane.astype(F32)
    rows = lax.broadcasted_iota(jnp.int32, (MOBA_BLOCK, MOBA_BLOCK), 0)
    cols = lax.broadcasted_iota(jnp.int32, (MOBA_BLOCK, MOBA_BLOCK), 1)
    nt_dims = (((1,), (1,)), ((), ()))
    heads = []
    for hh in range(2):
        hm = (lane >= hh * ATT_DH) & (lane < (hh + 1) * ATT_DH)
        qh = jnp.where(hm, q, jnp.zeros_like(q))
        gates = lax.dot_general(qh, km, nt_dims, preferred_element_type=F32)
        cur = jnp.where(lane < qi, gates, -jnp.inf)
        selmask = jnp.zeros((MOBA_BLOCK, LANES), F32)
        for _ in range(MOBA_TOPK):
            m = jnp.max(cur, axis=1, keepdims=True)
            idx = jnp.min(jnp.where(cur == m, lanef, float(LANES)), axis=1, keepdims=True)
            selmask = jnp.where((lanef == idx) & (m > -jnp.inf), 1.0, selmask)
            cur = jnp.where(lanef == idx, -jnp.inf, cur)
        own0 = pl.multiple_of(qi * MOBA_BLOCK, MOBA_BLOCK)
        s = lax.dot_general(qh, k_ref[pl.ds(own0, MOBA_BLOCK), :], nt_dims, preferred_element_type=F32)
        s = jnp.where(cols <= rows, s, -jnp.inf)
        m0 = jnp.max(s, axis=1, keepdims=True)
        p = jnp.exp(s - m0)
        l0 = jnp.sum(p, axis=1, keepdims=True)
        acc0 = jnp.dot(p.astype(BF16), v_ref[pl.ds(own0, MOBA_BLOCK), :], preferred_element_type=F32)
        heads.append((qh, selmask, m0, l0, acc0))

    def body(nn, carry):
        r0 = pl.multiple_of(nn * MOBA_BLOCK, MOBA_BLOCK)
        kblk = k_ref[pl.ds(r0, MOBA_BLOCK), :]
        vblk = v_ref[pl.ds(r0, MOBA_BLOCK), :]
        out = []
        for hh in range(2):
            qh, selmask = heads[hh][0], heads[hh][1]
            m, l, acc = carry[hh]
            flag = jnp.sum(jnp.where(lane == nn, selmask, 0.0), axis=1, keepdims=True)
            s = lax.dot_general(qh, kblk, nt_dims, preferred_element_type=F32)
            s = jnp.where(flag > 0.5, s, -jnp.inf)
            mn = jnp.maximum(m, jnp.max(s, axis=1, keepdims=True))
            a = jnp.exp(m - mn)
            p = jnp.exp(s - mn)
            l = a * l + jnp.sum(p, axis=1, keepdims=True)
            acc = a * acc + jnp.dot(p.astype(BF16), vblk, preferred_element_type=F32)
            out.append((mn, l, acc))
        return tuple(out)

    fin = lax.fori_loop(0, qi, body, tuple((hd[2], hd[3], hd[4]) for hd in heads))
    o0 = fin[0][2] / fin[0][1]
    o1 = fin[1][2] / fin[1][1]
    o_ref[...] = jnp.where(lane < ATT_DH, o0, o1).astype(BF16)


def moba_attn(q, kb, vb, kmean, nb, s_len):
    nblk = s_len // MOBA_BLOCK
    assert nblk <= LANES
    hp = D // LANES
    kmean = jnp.pad(kmean, ((0, 0), (0, LANES - nblk), (0, 0)))
    return pl.pallas_call(
        _moba_attn_kernel,
        grid=(nb, hp, nblk),
        in_specs=[pl.BlockSpec((MOBA_BLOCK, LANES), lambda b, h, i: (b * nblk + i, h)),
                  pl.BlockSpec((s_len, LANES), lambda b, h, i: (b, h)),
                  pl.BlockSpec((s_len, LANES), lambda b, h, i: (b, h)),
                  pl.BlockSpec((None, LANES, LANES), lambda b, h, i: (b, 0, h))],
        out_specs=pl.BlockSpec((MOBA_BLOCK, LANES), lambda b, h, i: (b * nblk + i, h)),
        out_shape=jax.ShapeDtypeStruct((nb * s_len, D), BF16),
        compiler_params=_cparams(("arbitrary", "arbitrary", "arbitrary")),
        name="moba_attn",
    )(q, kb, vb, kmean)


def _page_gate_kernel(pt_ref, *refs, n_new):
    k_refs, qb_ref, o_ref, gacc = refs[:PAGES_PER_STEP], refs[PAGES_PER_STEP], refs[PAGES_PER_STEP + 1], refs[PAGES_PER_STEP + 2]
    s = pl.program_id(1)
    ppb = MOBA_BLOCK // PAGE
    lane = lax.broadcasted_iota(jnp.int32, (ATT_HEADS, 1, LANES), 2)

    @pl.when(s == 0)
    def _init():
        gacc[...] = jnp.full(gacc.shape, -jnp.inf, F32)

    for mb in range(PAGES_PER_STEP // ppb):
        ksum = k_refs[ppb * mb][...]
        for pp in range(1, ppb):
            ksum = ksum + k_refs[ppb * mb + pp][...]
        blk = s * (PAGES_PER_STEP // ppb) + mb
        for t in range(n_new):
            prod = ksum * qb_ref[t]
            col = jnp.sum(jnp.sum(prod, axis=1, keepdims=True), axis=2, keepdims=True) * (1.0 / MOBA_BLOCK)
            gacc[t] = jnp.where(lane == blk, col, gacc[t])

    @pl.when(s == pl.num_programs(1) - 1)
    def _select():
        lanef = lane.astype(F32)
        out = jnp.zeros((ATT_HEADS, 1, LANES), F32)
        for t in range(n_new):
            cur = gacc[t]
            for j in range(MOBA_TOPK):
                m = jnp.max(cur, axis=2, keepdims=True)
                idx = jnp.min(jnp.where(cur == m, lanef, float(LANES)), axis=2, keepdims=True)
                out = jnp.where(lane == t * 4 + j, idx, out)
                cur = jnp.where(lanef == idx, -jnp.inf, cur)
        o_ref[...] = out


def page_gates(cache_kt, page_table, qb, n_new, layer):
    db, npages = page_table.shape
    assert npages % PAGES_PER_STEP == 0 and npages * PAGE // MOBA_BLOCK <= LANES and n_new <= 4

    def page_spec(u):
        return pl.BlockSpec((None, None, ATT_HEADS, ATT_DH, PAGE),
                            lambda b, s, pt: (layer, pt[b, s * PAGES_PER_STEP + u], 0, 0, 0))

    return pl.pallas_call(
        functools.partial(_page_gate_kernel, n_new=n_new),
        grid_spec=pltpu.PrefetchScalarGridSpec(
            num_scalar_prefetch=1,
            grid=(db, npages // PAGES_PER_STEP),
            in_specs=[page_spec(u) for u in range(PAGES_PER_STEP)]
            + [pl.BlockSpec((None, n_new, ATT_HEADS, ATT_DH, PAGE), lambda b, s, pt: (b, 0, 0, 0, 0))],
            out_specs=pl.BlockSpec((None, ATT_HEADS, 1, LANES), lambda b, s, pt: (b, 0, 0, 0)),
            scratch_shapes=[pltpu.VMEM((n_new, ATT_HEADS, 1, LANES), F32)],
        ),
        out_shape=jax.ShapeDtypeStruct((db, ATT_HEADS, 1, LANES), F32),
        compiler_params=_cparams(("arbitrary", "arbitrary")),
        name="moba_page_gates",
    )(page_table, *([cache_kt] * PAGES_PER_STEP), qb)


def _sample_attn_kernel(phys_ref, q_ref, kn_ref, vn_ref, ck_ref, cv_ref, o_ref, kbuf, vbuf, sems, *, n_new, layer):
    step = pl.program_id(0) * ATT_HEADS + pl.program_id(1)
    nsteps = pl.num_programs(0) * ATT_HEADS
    ppb = MOBA_BLOCK // PAGE
    per_tok = MOBA_TOPK * ppb
    per_step = n_new * per_tok

    def copies(st):
        hh = st % ATT_HEADS
        slot = st % 2
        out = []
        for u in range(per_step):
            ph = phys_ref[st * per_step + u]
            out.append(pltpu.make_async_copy(ck_ref.at[layer, ph, hh], kbuf.at[slot, u], sems.at[slot]))
            out.append(pltpu.make_async_copy(cv_ref.at[layer, ph, hh], vbuf.at[slot, u], sems.at[slot]))
        return out

    @pl.when(step == 0)
    def _first():
        for cp in copies(step):
            cp.start()

    @pl.when(step + 1 < nsteps)
    def _next():
        for cp in copies(step + 1):
            cp.start()

    for cp in copies(step):
        cp.wait()

    slot = step % 2
    q = q_ref[...]
    kn = _bf16_round(kn_ref[...])
    vn = _bf16_round(vn_ref[...])
    lane1 = lax.broadcasted_iota(jnp.int32, (1, LANES), 1)
    lane = lax.broadcasted_iota(jnp.int32, (ATT_DH, LANES), 1)
    out = jnp.zeros((ATT_DH, LANES), F32)
    for t in range(n_new):
        qc = q[:, t:t + 1]
        kk = _bf16_round(kbuf[slot, pl.ds(t * per_tok, per_tok)])
        vv = _bf16_round(vbuf[slot, pl.ds(t * per_tok, per_tok)])
        s = jnp.sum(kk * qc[None], axis=1, keepdims=True)
        so = jnp.sum(kn * qc, axis=0, keepdims=True)
        so = jnp.where(lane1 <= t, so, -jnp.inf)
        ms = jnp.max(jnp.max(s, axis=0, keepdims=True), axis=2, keepdims=True)
        m = jnp.maximum(ms[0], jnp.max(so, axis=1, keepdims=True))
        p = jnp.exp(s - m[None])
        po = jnp.exp(so - m)
        den = (jnp.sum(jnp.sum(p, axis=0, keepdims=True), axis=2, keepdims=True)[0]
               + jnp.sum(po, axis=1, keepdims=True))
        pv = jnp.sum(vv * _bf16_round(p / den[None]), axis=0)
        num = jnp.sum(pv, axis=1, keepdims=True) + jnp.sum(vn * _bf16_round(po / den), axis=1, keepdims=True)
        out = jnp.where(lane == t, num, out)
    o_ref[...] = out


def sample_attn(phys, q_t, kn_t, vn_t, cache_kt, cache_vt, n_new, layer):
    db = q_t.shape[0]
    hspec = pl.BlockSpec((None, None, ATT_DH, LANES), lambda b, h, ph: (b, h, 0, 0))
    pages = n_new * MOBA_TOPK * (MOBA_BLOCK // PAGE)
    return pl.pallas_call(
        functools.partial(_sample_attn_kernel, n_new=n_new, layer=layer),
        grid_spec=pltpu.PrefetchScalarGridSpec(
            num_scalar_prefetch=1,
            grid=(db, ATT_HEADS),
            in_specs=[hspec, hspec, hspec, pl.BlockSpec(memory_space=pl.ANY), pl.BlockSpec(memory_space=pl.ANY)],
            out_specs=hspec,
            scratch_shapes=[pltpu.VMEM((2, pages, ATT_DH, PAGE), F32), pltpu.VMEM((2, pages, ATT_DH, PAGE), F32),
                            pltpu.SemaphoreType.DMA((2,))],
        ),
        out_shape=jax.ShapeDtypeStruct((db, ATT_HEADS, ATT_DH, LANES), F32),
        compiler_params=_cparams(("arbitrary", "arbitrary")),
        name="moba_sample_attn",
    )(phys, q_t, kn_t, vn_t, cache_kt, cache_vt)


def _rot_tables(pos, theta, rot_dims):
    half = rot_dims // 2
    inv = jnp.power(theta, -jnp.arange(half, dtype=F32) / half)
    ang = pos.astype(F32)[:, None] * inv[None, :]
    return jnp.cos(ang), jnp.sin(ang)


def _positions(s_len, past_len, n_new, n_sample_rows):
    r = jnp.arange(n_sample_rows) % SP
    return jnp.concatenate([jnp.arange(s_len), past_len + jnp.minimum(r, n_new - 1)])


def _attn_rot_tables(pos):
    cos, sin = _rot_tables(pos, ROPE_THETA, ROT_DIMS)
    half = ROT_DIMS // 2
    n = pos.shape[0]
    one = jnp.ones((n, ATT_DH - ROT_DIMS), F32)
    zero8 = jnp.zeros((n, half), F32)
    zrest = jnp.zeros((n, ATT_DH - ROT_DIMS), F32)
    c64 = jnp.concatenate([cos, cos, one], axis=1)
    s1 = jnp.concatenate([-sin, zero8, zrest], axis=1)
    s2 = jnp.concatenate([zero8, sin, zrest], axis=1)
    rep = lambda a: jnp.concatenate([a, a], axis=1).reshape(n // TM, TM, LANES)
    return rep(c64), rep(s1), rep(s2)


def _ret_tables(l_real, l_pad):
    lg = jnp.log1p(-jnp.exp2(-5.0 - jnp.arange(RET_HEADS, dtype=F32)))
    idx = jnp.arange(l_real, dtype=F32)
    diff = idx[:, None] - idx[None, :]
    dmat = jnp.where(diff >= 0, jnp.exp(jnp.maximum(diff, 0.0)[None] * lg[:, None, None]), 0.0)
    q_decay = jnp.exp((idx[:, None] + 1.0) * lg[None, :]).T
    k_decay = jnp.exp((l_real - 1.0 - idx)[:, None] * lg[None, :]).T
    pad = l_pad - l_real
    dmat = jnp.pad(dmat, ((0, 0), (0, pad), (0, pad)))
    q_decay = jnp.pad(q_decay, ((0, 0), (0, pad)))[:, :, None]
    k_decay = jnp.pad(k_decay, ((0, 0), (0, pad)))[:, :, None]
    s_decay = jnp.broadcast_to(jnp.exp(l_real * lg)[:, None, None], (RET_HEADS, 1, RET_DV))
    return dmat, q_decay, k_decay, s_decay


def _mod_tables(m, nb, db):
    out = []
    for j in range(6):
        mj = m[:, j * D:(j + 1) * D]
        mp = jnp.broadcast_to(mj[:nb, None, :], (nb, TM, D))
        ms = jnp.repeat(mj[nb:nb + db], SP, axis=0).reshape(-1, TM, D)
        out.append(jnp.concatenate([mp, ms], axis=0))
    return out


def kernel(x_prompt, x_sample, c_prompt, c_sample, state_ret, cache_k, cache_v, page_table, ret_w_in, ret_gn_g, ret_w_out, att_w_qkv, att_w_out, ada_w, ada_b, ln_mix_g, ln_mix_b, ln_ffn_g, ln_ffn_b, router_w, router_b, exp_w_gu, exp_b_gu, exp_w_down, exp_b_down):
    nb, s_len, _ = x_prompt.shape
    db, n_new, _ = x_sample.shape
    past_len = page_table.shape[1] * PAGE
    assert past_len % MOBA_BLOCK == 0 and s_len % TM == 0 and (db * SP) % TM == 0 and n_new <= SP
    tp, ts = nb * s_len, db * SP
    npt, nst, tpb = tp // TM, ts // TM, s_len // TM
    nt = npt + nst
    geo = {
        "nt": nt,
        "mod_idx": lambda i: jnp.where(i < npt, i // tpb, nb + i - npt),
        "tab_idx": lambda i: jnp.where(i < npt, i % tpb, tpb + i - npt),
    }

    x = jnp.concatenate([x_prompt.reshape(tp, D),
                         jnp.pad(x_sample, ((0, 0), (0, SP - n_new), (0, 0))).reshape(ts, D)], axis=0)
    mods = ada_mods(jnp.concatenate([c_prompt, c_sample], axis=0), ada_w, ada_b)

    pos = _positions(s_len, past_len, n_new, ts)
    rcos, rsin = _rot_tables(pos, RET_THETA, RET_DK)
    rcos, rsin = rcos.reshape(-1, TM, LANES), rsin.reshape(-1, TM, LANES)
    atabs = _attn_rot_tables(pos)
    wr_pad = jnp.pad(router_w, ((0, 0), (0, 0), (0, LANES - N_EXPERTS))).astype(BF16)
    br_pad = jnp.pad(router_b, ((0, 0), (0, LANES - N_EXPERTS)), constant_values=NEG)
    cache_kt = cache_k.transpose(0, 1, 3, 4, 2)
    cache_vt = cache_v.transpose(0, 1, 3, 4, 2)

    st_p, st_s, kp_l, vp_l, ks_l, vs_l = [], [], [], [], [], []
    for i in range(DEPTH):
        j = i // 2
        m = _mod_tables(mods[i], nb, db)
        if i % 2 == 0:
            q, k, v, g = ret_proj(x, m[0], m[1], ret_w_in[j].astype(BF16), rcos, rsin, geo)
            yp, sp_state = ret_chunks(q, k, v, g, _ret_tables(RET_CHUNK, RET_CHUNK), ret_gn_g[j], None,
                                      nb, s_len // RET_CHUNK, RET_CHUNK, 0, BF16)
            ys, ss_state = ret_chunks(q[tp:].astype(F32), k[tp:], v[tp:].astype(F32), g[tp:], _ret_tables(n_new, SP),
                                      ret_gn_g[j], state_ret[j], db, 1, SP, 0, F32)
            y = jnp.concatenate([yp, ys.astype(BF16)], axis=0)
            w_out = ret_w_out[j].astype(BF16)
            st_p.append(sp_state)
            st_s.append(ss_state)
        else:
            q, kf, kb, vf, vb, kmean = attn_proj(x, m[0], m[1], att_w_qkv[j].astype(BF16), *atabs, geo)
            op = moba_attn(q, kb, vb, kmean[:npt].reshape(nb, tpb, D), nb, s_len)
            to_heads = lambda a: a[tp:].astype(F32).reshape(db, SP, ATT_HEADS, ATT_DH).transpose(0, 2, 3, 1)
            lane_pad = lambda a: jnp.pad(a, ((0, 0), (0, 0), (0, 0), (0, LANES - SP)))
            qh = to_heads(q)
            qb = jnp.broadcast_to(qh.transpose(0, 3, 1, 2)[:, :n_new, :, :, None], (db, n_new, ATT_HEADS, ATT_DH, PAGE))
            sel = page_gates(cache_kt, page_table, qb, n_new, j)
            sel = sel.reshape(db, ATT_HEADS, LANES)[:, :, :16].reshape(db, ATT_HEADS, 4, 4)
            sel = sel[:, :, :n_new, :MOBA_TOPK].astype(jnp.int32)
            ppb = MOBA_BLOCK // PAGE
            lpage = sel[..., None] * ppb + jnp.arange(ppb, dtype=jnp.int32)
            phys = page_table[jnp.arange(db)[:, None, None, None, None], lpage].reshape(-1).astype(jnp.int32)
            oh = sample_attn(phys, lane_pad(qh), lane_pad(to_heads(kf)), lane_pad(to_heads(vf)),
                             cache_kt, cache_vt, n_new, j)
            os_rows = oh[..., :SP].transpose(0, 3, 1, 2).reshape(ts, D).astype(BF16)
            y = jnp.concatenate([op, os_rows], axis=0)
            w_out = att_w_out[j].astype(BF16)
            shp_p = (nb, s_len, ATT_HEADS, ATT_DH)
            shp_s = (db, SP, ATT_HEADS, ATT_DH)
            kp_l.append(kf[:tp].reshape(shp_p))
            vp_l.append(vf[:tp].reshape(shp_p))
            ks_l.append(kf[tp:].reshape(shp_s)[:, :n_new])
            vs_l.append(vf[tp:].reshape(shp_s)[:, :n_new])
        x, t_rows, ti_f, tg = outproj_router(y, w_out, x, m[2], ln_mix_g[i], ln_mix_b[i], m[3], m[4],
                                             wr_pad[i], br_pad[i].reshape(1, LANES), geo)
        x = moe_ffn_norm(t_rows, ti_f, tg, x, m[5], ln_ffn_g[i], ln_ffn_b[i],
                         exp_w_gu, exp_b_gu, exp_w_down, exp_b_down, i, geo)

    y_prompt = x[:tp].reshape(nb, s_len, D)
    y_sample = x[tp:].reshape(db, SP, D)[:, :n_new]
    return (y_prompt, y_sample, jnp.stack(st_p), jnp.stack(st_s), jnp.stack(kp_l), jnp.stack(vp_l),
            jnp.stack(ks_l), jnp.stack(vs_l))
```

```python
import functools

import jax
import jax.numpy as jnp
from jax import lax
from jax.experimental import pallas as pl
from jax.experimental.pallas import tpu as pltpu

F32 = jnp.float32
BF16 = jnp.bfloat16

D = 1024
DEPTH = 2
RET_HEADS, RET_DK, RET_DV, RET_CHUNK, RET_THETA = 4, 256, 512, 128, 10000.0
RET_COLS = 2 * RET_HEADS * RET_DK + 2 * RET_HEADS * RET_DV
ATT_HEADS, ATT_DH, ROT_DIMS, ROPE_THETA = 16, 64, 16, 500000.0
MOBA_BLOCK, MOBA_TOPK, PAGE = 256, 3, 128
N_EXPERTS, TOP_K, D_FF = 32, 4, 1024
SWIGLU_LIMIT, SWIGLU_ALPHA = 7.0, 1.702
DN_ALPHA = (2 * DEPTH) ** 0.25
LN_EPS = 1e-5

TM = 256
SP = 8
TE = 256
PAGES_PER_STEP = 8
FF_CHUNKS = 4
CS = 128
LANES = 128
VMEM_LIMIT = 56 * 1024 * 1024
NEG = -1e30


def _cparams(sem, **kw):
    return pltpu.CompilerParams(dimension_semantics=sem, vmem_limit_bytes=VMEM_LIMIT, **kw)


def _bf16_round(x):
    return x.astype(BF16).astype(F32)


def _layer_norm(r, g, b):
    mu = jnp.mean(r, axis=-1, keepdims=True)
    c = r - mu
    var = jnp.mean(c * c, axis=-1, keepdims=True)
    return c * lax.rsqrt(var + LN_EPS) * g + b


def _ada_kernel(c_ref, w_ref, b_ref, o_ref):
    c = c_ref[...]
    s = (c * jax.nn.sigmoid(c)).astype(BF16)
    o_ref[...] = jnp.dot(s, w_ref[...].astype(BF16), preferred_element_type=F32) + b_ref[...]


def ada_mods(c_all, ada_w, ada_b):
    nl, _, ncol = ada_w.shape
    nc = c_all.shape[0]
    tn = 1536
    return pl.pallas_call(
        _ada_kernel,
        grid=(nl, ncol // tn),
        in_specs=[pl.BlockSpec((nc, D), lambda l, j: (0, 0)),
                  pl.BlockSpec((None, D, tn), lambda l, j: (l, 0, j)),
                  pl.BlockSpec((None, 1, tn), lambda l, j: (l, 0, j))],
        out_specs=pl.BlockSpec((None, nc, tn), lambda l, j: (l, 0, j)),
        out_shape=jax.ShapeDtypeStruct((nl, nc, ncol), F32),
        compiler_params=_cparams(("arbitrary", "arbitrary")),
        name="ada_mods",
    )(c_all, ada_w, ada_b.reshape(nl, 1, ncol))


def _ret_proj_kernel(x_ref, sh_ref, sc_ref, w_ref, cos_ref, sin_ref, q_ref, k_ref, v_ref, g_ref):
    h = (x_ref[...] * (1.0 + sc_ref[...]) + sh_ref[...]).astype(BF16)
    cos = cos_ref[...]
    sin = sin_ref[...]
    nq = RET_HEADS * RET_DK
    nv = RET_HEADS * RET_DV
    half = RET_DK // 2
    zq = jnp.dot(h, w_ref[:, 0:nq], preferred_element_type=F32)
    zk = jnp.dot(h, w_ref[:, nq:2 * nq], preferred_element_type=F32)
    for hh in range(RET_HEADS):
        lo = hh * RET_DK
        x1, x2 = zq[:, lo:lo + half], zq[:, lo + half:lo + RET_DK]
        q_ref[:, lo:lo + half] = (x1 * cos - x2 * sin).astype(BF16)
        q_ref[:, lo + half:lo + RET_DK] = (x2 * cos + x1 * sin).astype(BF16)
        y1, y2 = zk[:, lo:lo + half], zk[:, lo + half:lo + RET_DK]
        k_ref[:, lo:lo + half] = (y1 * cos - y2 * sin) * (RET_DK ** -0.5)
        k_ref[:, lo + half:lo + RET_DK] = (y2 * cos + y1 * sin) * (RET_DK ** -0.5)
    v_ref[...] = jnp.dot(h, w_ref[:, 2 * nq:2 * nq + nv], preferred_element_type=F32).astype(BF16)
    g_ref[...] = jnp.dot(h, w_ref[:, 2 * nq + nv:], preferred_element_type=F32)


def ret_proj(x, sh, sc, w_bf, cos_t, sin_t, geo):
    nt, mod_idx, tab_idx = geo["nt"], geo["mod_idx"], geo["tab_idx"]
    t = x.shape[0]
    nq = RET_HEADS * RET_DK
    nv = RET_HEADS * RET_DV
    row = lambda w: pl.BlockSpec((TM, w), lambda i: (i, 0))
    mod = pl.BlockSpec((None, TM, D), lambda i: (mod_idx(i), 0, 0))
    tab = pl.BlockSpec((None, TM, LANES), lambda i: (tab_idx(i), 0, 0))
    return pl.pallas_call(
        _ret_proj_kernel,
        grid=(nt,),
        in_specs=[row(D), mod, mod, pl.BlockSpec((D, RET_COLS), lambda i: (0, 0)), tab, tab],
        out_specs=[row(nq), row(nq), row(nv), row(nv)],
        out_shape=[jax.ShapeDtypeStruct((t, nq), BF16), jax.ShapeDtypeStruct((t, nq), F32),
                   jax.ShapeDtypeStruct((t, nv), BF16), jax.ShapeDtypeStruct((t, nv), F32)],
        compiler_params=_cparams(("arbitrary",)),
        name="ret_proj",
    )(x, sh, sc, w_bf, cos_t, sin_t)


def _ret_chunk_kernel(*refs, has_state):
    if has_state:
        q_ref, k_ref, v_ref, g_ref, dm_ref, qd_ref, kd_ref, sd_ref, gn_ref, st0_ref, y_ref, st_ref = refs
    else:
        q_ref, k_ref, v_ref, g_ref, dm_ref, qd_ref, kd_ref, sd_ref, gn_ref, y_ref, st_ref = refs
    c = pl.program_id(2)

    @pl.when(c == 0)
    def _init():
        if has_state:
            st_ref[...] = st0_ref[...]
        else:
            st_ref[...] = jnp.zeros(st_ref.shape, F32)

    q = q_ref[...].astype(BF16)
    kf = k_ref[...]
    v = v_ref[...].astype(BF16)
    st = st_ref[...]
    scores = lax.dot_general(q, kf.astype(BF16), (((1,), (1,)), ((), ())), preferred_element_type=F32) * dm_ref[...]
    inner = jnp.dot(scores.astype(BF16), v, preferred_element_type=F32)
    cross = jnp.dot(q, st.astype(BF16), preferred_element_type=F32) * qd_ref[...]
    o = inner + cross
    kd = (kf * kd_ref[...]).astype(BF16)
    upd = lax.dot_general(kd, v, (((0,), (0,)), ((), ())), preferred_element_type=F32)
    st_ref[...] = sd_ref[...] * st + upd
    mu = jnp.mean(o, axis=-1, keepdims=True)
    cen = o - mu
    var = jnp.mean(cen * cen, axis=-1, keepdims=True)
    on = cen * lax.rsqrt(var + LN_EPS) * gn_ref[...]
    gg = g_ref[...]
    y_ref[...] = (gg * jax.nn.sigmoid(gg) * on).astype(y_ref.dtype)


def ret_chunks(q, k, v, g, tabs, gn_g, state0, nb, nc, blk_rows, row_blk0, y_dtype):
    dm, qd, kd, sd = tabs
    L = blk_rows
    has_state = state0 is not None
    rb = lambda b, h, c: (row_blk0 + b * nc + c, h)
    head3 = lambda shp: pl.BlockSpec((None,) + shp, lambda b, h, c: (h, 0, 0))
    in_specs = [pl.BlockSpec((L, RET_DK), rb), pl.BlockSpec((L, RET_DK), rb),
                pl.BlockSpec((L, RET_DV), rb), pl.BlockSpec((L, RET_DV), rb),
                head3((L, L)), head3((L, 1)), head3((L, 1)), head3((1, RET_DV)), head3((1, RET_DV))]
    args = [q, k, v, g, dm, qd, kd, sd, gn_g.reshape(RET_HEADS, 1, RET_DV)]
    st_spec = pl.BlockSpec((None, None, RET_DK, RET_DV), lambda b, h, c: (b, h, 0, 0))
    if has_state:
        in_specs.append(st_spec)
        args.append(state0)
    return pl.pallas_call(
        functools.partial(_ret_chunk_kernel, has_state=has_state),
        grid=(nb, RET_HEADS, nc),
        in_specs=in_specs,
        out_specs=[pl.BlockSpec((L, RET_DV), lambda b, h, c: (b * nc + c, h)), st_spec],
        out_shape=[jax.ShapeDtypeStruct((nb * nc * L, RET_HEADS * RET_DV), y_dtype),
                   jax.ShapeDtypeStruct((nb, RET_HEADS, RET_DK, RET_DV), F32)],
        compiler_params=_cparams(("arbitrary", "arbitrary", "arbitrary")),
        name="ret_chunks_state" if has_state else "ret_chunks",
    )(*args)


def _outproj_router_kernel(y_ref, w_ref, x_ref, gate_ref, lng_ref, lnb_ref, sh_ref, sc_ref, wr_ref, br_ref,
                           xn_ref, t_ref, ti_ref, tg_ref):
    f = jnp.dot(y_ref[...], w_ref[...], preferred_element_type=F32)
    r = DN_ALPHA * x_ref[...] + (1.0 + gate_ref[...]) * f
    xn = _layer_norm(r, lng_ref[...], lnb_ref[...])
    xn_ref[...] = xn
    t = xn * (1.0 + sc_ref[...]) + sh_ref[...]
    t_ref[...] = t
    logits = jnp.dot(t.astype(BF16), wr_ref[...], preferred_element_type=F32) + br_ref[...]
    lane = lax.broadcasted_iota(jnp.int32, logits.shape, 1).astype(F32)
    cur = logits
    vals, idxs = [], []
    for _ in range(TOP_K):
        m = jnp.max(cur, axis=1, keepdims=True)
        idx = jnp.min(jnp.where(cur == m, lane, float(LANES)), axis=1, keepdims=True)
        vals.append(m)
        idxs.append(idx)
        cur = jnp.where(lane == idx, -jnp.inf, cur)
    es = [jnp.exp(vv - vals[0]) for vv in vals]
    den = es[0] + es[1] + es[2] + es[3]
    ti = jnp.zeros(logits.shape, F32)
    tg = jnp.zeros(logits.shape, F32)
    for kk in range(TOP_K):
        ti = jnp.where(lane == float(kk), idxs[kk], ti)
        tg = jnp.where(lane == float(kk), es[kk] / den, tg)
    ti_ref[...] = ti
    tg_ref[...] = tg


def outproj_router(y, w_bf, x, gate, lng, lnb, sh, sc, wr_pad, br_pad, geo):
    nt, mod_idx = geo["nt"], geo["mod_idx"]
    t, kdim = y.shape
    row = lambda w: pl.BlockSpec((TM, w), lambda i: (i, 0))
    mod = pl.BlockSpec((None, TM, D), lambda i: (mod_idx(i), 0, 0))
    vec = lambda w: pl.BlockSpec((1, w), lambda i: (0, 0))
    return pl.pallas_call(
        _outproj_router_kernel,
        grid=(nt,),
        in_specs=[row(kdim), pl.BlockSpec((kdim, D), lambda i: (0, 0)), row(D), mod, vec(D), vec(D), mod, mod,
                  pl.BlockSpec((D, LANES), lambda i: (0, 0)), vec(LANES)],
        out_specs=[row(D), row(D), row(LANES), row(LANES)],
        out_shape=[jax.ShapeDtypeStruct((t, D), F32), jax.ShapeDtypeStruct((t, D), F32),
                   jax.ShapeDtypeStruct((t, LANES), F32), jax.ShapeDtypeStruct((t, LANES), F32)],
        compiler_params=_cparams(("arbitrary",)),
        name="outproj_router",
    )(y, w_bf, x, gate, lng.reshape(1, D), lnb.reshape(1, D), sh, sc, wr_pad, br_pad)


def _expert_kernel(be_ref, idx_ref, nxt_ref, src_ref, wgu_ref, bgu_ref, wdn_ref, bdn_ref, o_ref,
                   xbuf, wgu_bf, bgu_ch, wdn_bf, sems):
    i = pl.program_id(0)
    n = pl.num_programs(0)
    slot = i % 2
    cw = D_FF // FF_CHUNKS
    rows_per_chunk = TE // FF_CHUNKS

    def row_copy(sl, j, tok):
        return pltpu.make_async_copy(src_ref.at[pl.ds(tok, 1)], xbuf.at[sl, pl.ds(j, 1)], sems.at[sl])

    @pl.when(i == 0)
    def _first():
        def body(j, c):
            row_copy(slot, j, idx_ref[0, 0, j]).start()
            return c
        lax.fori_loop(0, TE, body, 0, unroll=8)

    new_expert = (i == 0) | (be_ref[i] != be_ref[jnp.maximum(i - 1, 0)])

    @pl.when(new_expert)
    def _cast_weights():
        for c in range(FF_CHUNKS):
            lo, hi = c * cw, (c + 1) * cw
            wgu_bf[c, :, 0:cw] = wgu_ref[:, lo:hi].astype(BF16)
            wgu_bf[c, :, cw:2 * cw] = wgu_ref[:, D_FF + lo:D_FF + hi].astype(BF16)
            bgu_ch[c, :, 0:cw] = bgu_ref[:, lo:hi]
            bgu_ch[c, :, cw:2 * cw] = bgu_ref[:, D_FF + lo:D_FF + hi]
            wdn_bf[c] = wdn_ref[lo:hi, :].astype(BF16)

    pltpu.make_async_copy(src_ref.at[pl.ds(0, TE)], xbuf.at[slot], sems.at[slot]).wait()
    o_ref[...] = jnp.broadcast_to(bdn_ref[...], o_ref.shape)

    def chunk(c, carry, prefetch):
        x = xbuf[slot].astype(BF16)
        if prefetch:
            for jj in range(rows_per_chunk):
                j = c * rows_per_chunk + jj
                row_copy(1 - slot, j, nxt_ref[0, 0, j]).start()
        gu = jnp.dot(x, wgu_bf[c], preferred_element_type=F32) + bgu_ch[c]
        g = jnp.minimum(gu[:, :cw], SWIGLU_LIMIT)
        u = jnp.clip(gu[:, cw:], -SWIGLU_LIMIT, SWIGLU_LIMIT)
        a = (u + 1.0) * g * jax.nn.sigmoid(SWIGLU_ALPHA * g)
        o_ref[...] += jnp.dot(a.astype(BF16), wdn_bf[c], preferred_element_type=F32)
        return carry

    @pl.when(i + 1 < n)
    def _with_prefetch():
        lax.fori_loop(0, FF_CHUNKS, functools.partial(chunk, prefetch=True), 0)

    @pl.when(i + 1 == n)
    def _last():
        lax.fori_loop(0, FF_CHUNKS, functools.partial(chunk, prefetch=False), 0)


def expert_blocks(t_rows, slot_tok, block_exp, wgu, bgu, wdn, bdn, layer, n_blocks):
    smem = lambda f: pl.BlockSpec((1, 1, TE), f, memory_space=pltpu.SMEM)
    wspec = lambda r, c: pl.BlockSpec((None, None, r, c), lambda i, be: (layer, be[i], 0, 0))
    cw = D_FF // FF_CHUNKS
    return pl.pallas_call(
        _expert_kernel,
        grid_spec=pltpu.PrefetchScalarGridSpec(
            num_scalar_prefetch=1,
            grid=(n_blocks,),
            in_specs=[smem(lambda i, be: (i, 0, 0)),
                      smem(lambda i, be: (jnp.minimum(i + 1, n_blocks - 1), 0, 0)),
                      pl.BlockSpec(memory_space=pl.ANY),
                      wspec(D, 2 * D_FF), wspec(1, 2 * D_FF), wspec(D_FF, D), wspec(1, D)],
            out_specs=pl.BlockSpec((TE, D), lambda i, be: (i, 0)),
            scratch_shapes=[pltpu.VMEM((2, TE, D), F32), pltpu.VMEM((FF_CHUNKS, D, 2 * cw), BF16),
                            pltpu.VMEM((FF_CHUNKS, 1, 2 * cw), F32), pltpu.VMEM((FF_CHUNKS, cw, D), BF16),
                            pltpu.SemaphoreType.DMA((2,))],
        ),
        out_shape=jax.ShapeDtypeStruct((n_blocks * TE, D), F32),
        compiler_params=_cparams(("arbitrary",)),
        name="moe_expert_blocks",
    )(block_exp, slot_tok.reshape(n_blocks, 1, TE), slot_tok.reshape(n_blocks, 1, TE), t_rows,
      wgu, bgu.reshape(DEPTH, N_EXPERTS, 1, 2 * D_FF), wdn, bdn.reshape(DEPTH, N_EXPERTS, 1, D))


def _combine_norm_kernel(idx_ref, nxt_ref, src_ref, tg_ref, x_ref, gate_ref, lng_ref, lnb_ref, o_ref, buf, sems):
    i = pl.program_id(0)
    n = pl.num_programs(0)
    nrow = TOP_K * TM

    def row_copy(tile, j, row):
        return pltpu.make_async_copy(src_ref.at[pl.ds(row, 1)], buf.at[tile % 2, pl.ds(j, 1)], sems.at[tile % 2])

    def issue(tile, ref):
        def body(j, c):
            row_copy(tile, j, ref[0, 0, j]).start()
            return c
        lax.fori_loop(0, nrow, body, 0, unroll=8)

    @pl.when(i == 0)
    def _first():
        issue(i, idx_ref)

    @pl.when(i + 1 < n)
    def _next():
        issue(i + 1, nxt_ref)

    slot = i % 2
    pltpu.make_async_copy(src_ref.at[pl.ds(0, nrow)], buf.at[slot], sems.at[slot]).wait()

    tg = tg_ref[...]
    y = jnp.zeros((TM, D), F32)
    for kk in range(TOP_K):
        y = y + buf[slot, pl.ds(kk * TM, TM), :] * tg[:, kk:kk + 1]
    r = DN_ALPHA * x_ref[...] + (1.0 + gate_ref[...]) * y
    o_ref[...] = _layer_norm(r, lng_ref[...], lnb_ref[...])


def combine_norm(out_sorted, dest_tiles, tg, x, gate, lng, lnb, geo):
    nt, mod_idx = geo["nt"], geo["mod_idx"]
    t = x.shape[0]
    row = lambda w: pl.BlockSpec((TM, w), lambda i: (i, 0))
    vec = pl.BlockSpec((1, D), lambda i: (0, 0))
    smem = lambda f: pl.BlockSpec((1, 1, TOP_K * TM), f, memory_space=pltpu.SMEM)
    return pl.pallas_call(
        _combine_norm_kernel,
        grid=(nt,),
        in_specs=[smem(lambda i: (i, 0, 0)), smem(lambda i: (jnp.minimum(i + 1, nt - 1), 0, 0)),
                  pl.BlockSpec(memory_space=pl.ANY), row(LANES), row(D),
                  pl.BlockSpec((None, TM, D), lambda i: (mod_idx(i), 0, 0)), vec, vec],
        out_specs=row(D),
        out_shape=jax.ShapeDtypeStruct((t, D), F32),
        scratch_shapes=[pltpu.VMEM((2, TOP_K * TM, D), F32), pltpu.SemaphoreType.DMA((2,))],
        compiler_params=_cparams(("arbitrary",)),
        name="moe_combine_norm",
    )(dest_tiles, dest_tiles, out_sorted, tg, x, gate, lng.reshape(1, D), lnb.reshape(1, D))


def moe_ffn_norm(t_rows, ti_f, tg, x, gate, lng, lnb, wgu, bgu, wdn, bdn, layer, geo):
    t = t_rows.shape[0]
    nt = geo["nt"]
    ti = ti_f[:, :TOP_K].astype(jnp.int32)
    onehot = jnp.sum((ti[:, :, None] == jnp.arange(N_EXPERTS, dtype=jnp.int32)[None, None, :]).astype(jnp.int32), axis=1)
    oh3 = onehot.reshape(t // CS, CS, N_EXPERTS)
    tri = (jnp.arange(CS)[:, None] >= jnp.arange(CS)[None, :]).astype(F32)
    within = jnp.einsum("ij,cjk->cik", tri, oh3.astype(F32), preferred_element_type=F32).astype(jnp.int32)
    chunk_tot = within[:, -1, :]
    csum = (within + (jnp.cumsum(chunk_tot, axis=0) - chunk_tot)[:, None, :]).reshape(t, N_EXPERTS)
    counts = csum[-1]
    rank = jnp.take_along_axis(csum - onehot, ti, axis=1)
    padded = (counts + TE - 1) // TE * TE
    pad_end = jnp.cumsum(padded)
    pad_start = pad_end - padded
    dest = (pad_start[ti] + rank).astype(jnp.int32)
    n_blocks = (t * TOP_K) // TE + N_EXPERTS
    tok = jnp.broadcast_to(jnp.arange(t, dtype=jnp.int32)[:, None], (t, TOP_K))
    slot_tok = jnp.zeros((n_blocks * TE,), jnp.int32).at[dest.reshape(-1)].set(tok.reshape(-1))
    nvb = (pad_end[-1] // TE).astype(jnp.int32).reshape(1)
    blk_ids = jnp.arange(n_blocks, dtype=jnp.int32)
    block_exp = jnp.minimum(jnp.searchsorted(pad_end, blk_ids * TE, side="right"), N_EXPERTS - 1).astype(jnp.int32)
    block_exp = jnp.where(blk_ids < nvb[0], block_exp, block_exp[nvb[0] - 1])
    out_sorted = expert_blocks(t_rows, slot_tok, block_exp, wgu, bgu, wdn, bdn, layer, n_blocks)
    dest_tiles = dest.reshape(nt, TM, TOP_K).transpose(0, 2, 1).reshape(nt, 1, TOP_K * TM)
    return combine_norm(out_sorted, dest_tiles, tg, x, gate, lng, lnb, geo)


def _attn_proj_kernel(x_ref, sh_ref, sc_ref, w_ref, c_ref, s1_ref, s2_ref,
                      q_ref, kf_ref, kb_ref, vf_ref, vb_ref, km_ref):
    h = (x_ref[...] * (1.0 + sc_ref[...]) + sh_ref[...]).astype(BF16)
    cc, s1, s2 = c_ref[...], s1_ref[...], s2_ref[...]

    def rot(z):
        return z * cc + pltpu.roll(z, LANES - ROT_DIMS // 2, 1) * s1 + pltpu.roll(z, ROT_DIMS // 2, 1) * s2

    zq = jnp.dot(h, w_ref[:, 0:D], preferred_element_type=F32)
    zk = jnp.dot(h, w_ref[:, D:2 * D], preferred_element_type=F32)
    zv = jnp.dot(h, w_ref[:, 2 * D:3 * D], preferred_element_type=F32)
    for c in range(D // LANES):
        lo = c * LANES
        q_ref[:, lo:lo + LANES] = (rot(zq[:, lo:lo + LANES]) * (ATT_DH ** -0.5)).astype(BF16)
        kr = rot(zk[:, lo:lo + LANES])
        kf_ref[:, lo:lo + LANES] = kr
        kb_ref[:, lo:lo + LANES] = kr.astype(BF16)
        km_ref[:, lo:lo + LANES] = jnp.mean(kr, axis=0, keepdims=True)
    vf_ref[...] = zv
    vb_ref[...] = zv.astype(BF16)


def attn_proj(x, sh, sc, w_bf, ctab, s1tab, s2tab, geo):
    nt, mod_idx, tab_idx = geo["nt"], geo["mod_idx"], geo["tab_idx"]
    t = x.shape[0]
    row = pl.BlockSpec((TM, D), lambda i: (i, 0))
    mod = pl.BlockSpec((None, TM, D), lambda i: (mod_idx(i), 0, 0))
    tab = pl.BlockSpec((None, TM, LANES), lambda i: (tab_idx(i), 0, 0))
    return pl.pallas_call(
        _attn_proj_kernel,
        grid=(nt,),
        in_specs=[row, mod, mod, pl.BlockSpec((D, 3 * D), lambda i: (0, 0)), tab, tab, tab],
        out_specs=[row, row, row, row, row, pl.BlockSpec((None, 1, D), lambda i: (i, 0, 0))],
        out_shape=[jax.ShapeDtypeStruct((t, D), BF16), jax.ShapeDtypeStruct((t, D), F32),
                   jax.ShapeDtypeStruct((t, D), BF16), jax.ShapeDtypeStruct((t, D), F32),
                   jax.ShapeDtypeStruct((t, D), BF16), jax.ShapeDtypeStruct((nt, 1, D), F32)],
        compiler_params=_cparams(("arbitrary",)),
        name="attn_proj",
    )(x, sh, sc, w_bf, ctab, s1tab, s2tab)


def _moba_attn_kernel(q_ref, k_ref, v_ref, km_ref, o_ref):
    qi = pl.program_id(2)
    q = q_ref[...]
    km = km_ref[...].astype(BF16)
    lane = lax.broadcasted_iota(jnp.int32, (MOBA_BLOCK, LANES), 1)
    lanef = lane.astype(F32)
    rows = lax.broadcasted_iota(jnp.int32, (MOBA_BLOCK, MOBA_BLOCK), 0)
    cols = lax.broadcasted_iota(jnp.int32, (MOBA_BLOCK, MOBA_BLOCK), 1)
    nt_dims = (((1,), (1,)), ((), ()))
    heads = []
    for hh in range(2):
        hm = (lane >= hh * ATT_DH) & (lane < (hh + 1) * ATT_DH)
        qh = jnp.where(hm, q, jnp.zeros_like(q))
        gates = lax.dot_general(qh, km, nt_dims, preferred_element_type=F32)
        cur = jnp.where(lane < qi, gates, -jnp.inf)
        selmask = jnp.zeros((MOBA_BLOCK, LANES), F32)
        for _ in range(MOBA_TOPK):
            m = jnp.max(cur, axis=1, keepdims=True)
            idx = jnp.min(jnp.where(cur == m, lanef, float(LANES)), axis=1, keepdims=True)
            selmask = jnp.where((lanef == idx) & (m > -jnp.inf), 1.0, selmask)
            cur = jnp.where(lanef == idx, -jnp.inf, cur)
        own0 = pl.multiple_of(qi * MOBA_BLOCK, MOBA_BLOCK)
        s = lax.dot_general(qh, k_ref[pl.ds(own0, MOBA_BLOCK), :], nt_dims, preferred_element_type=F32)
        s = jnp.where(cols <= rows, s, -jnp.inf)
        m0 = jnp.max(s, axis=1, keepdims=True)
        p = jnp.exp(s - m0)
        l0 = jnp.sum(p, axis=1, keepdims=True)
        acc0 = jnp.dot(p.astype(BF16), v_ref[pl.ds(own0, MOBA_BLOCK), :], preferred_element_type=F32)
        heads.append((qh, selmask, m0, l0, acc0))

    def body(nn, carry):
        r0 = pl.multiple_of(nn * MOBA_BLOCK, MOBA_BLOCK)
        kblk = k_ref[pl.ds(r0, MOBA_BLOCK), :]
        vblk = v_ref[pl.ds(r0, MOBA_BLOCK), :]
        out = []
        for hh in range(2):
            qh, selmask = heads[hh][0], heads[hh][1]
            m, l, acc = carry[hh]
            flag = jnp.sum(jnp.where(lane == nn, selmask, 0.0), axis=1, keepdims=True)
            s = lax.dot_general(qh, kblk, nt_dims, preferred_element_type=F32)
            s = jnp.where(flag > 0.5, s, -jnp.inf)
            mn = jnp.maximum(m, jnp.max(s, axis=1, keepdims=True))
            a = jnp.exp(m - mn)
            p = jnp.exp(s - mn)
            l = a * l + jnp.sum(p, axis=1, keepdims=True)
            acc = a * acc + jnp.dot(p.astype(BF16), vblk, preferred_element_type=F32)
            out.append((mn, l, acc))
        return tuple(out)

    fin = lax.fori_loop(0, qi, body, tuple((hd[2], hd[3], hd[4]) for hd in heads))
    o0 = fin[0][2] / fin[0][1]
    o1 = fin[1][2] / fin[1][1]
    o_ref[...] = jnp.where(lane < ATT_DH, o0, o1).astype(BF16)


def moba_attn(q, kb, vb, kmean, nb, s_len):
    nblk = s_len // MOBA_BLOCK
    assert nblk <= LANES
    hp = D // LANES
    kmean = jnp.pad(kmean, ((0, 0), (0, LANES - nblk), (0, 0)))
    return pl.pallas_call(
        _moba_attn_kernel,
        grid=(nb, hp, nblk),
        in_specs=[pl.BlockSpec((MOBA_BLOCK, LANES), lambda b, h, i: (b * nblk + i, h)),
                  pl.BlockSpec((s_len, LANES), lambda b, h, i: (b, h)),
                  pl.BlockSpec((s_len, LANES), lambda b, h, i: (b, h)),
                  pl.BlockSpec((None, LANES, LANES), lambda b, h, i: (b, 0, h))],
        out_specs=pl.BlockSpec((MOBA_BLOCK, LANES), lambda b, h, i: (b * nblk + i, h)),
        out_shape=jax.ShapeDtypeStruct((nb * s_len, D), BF16),
        compiler_params=_cparams(("arbitrary", "arbitrary", "arbitrary")),
        name="moba_attn",
    )(q, kb, vb, kmean)


def _page_gate_kernel(pt_ref, *refs, n_new):
    k_refs, qb_ref, o_ref, gacc = refs[:PAGES_PER_STEP], refs[PAGES_PER_STEP], refs[PAGES_PER_STEP + 1], refs[PAGES_PER_STEP + 2]
    s = pl.program_id(1)
    ppb = MOBA_BLOCK // PAGE
    lane = lax.broadcasted_iota(jnp.int32, (ATT_HEADS, 1, LANES), 2)

    @pl.when(s == 0)
    def _init():
        gacc[...] = jnp.full(gacc.shape, -jnp.inf, F32)

    for mb in range(PAGES_PER_STEP // ppb):
        ksum = k_refs[ppb * mb][...]
        for pp in range(1, ppb):
            ksum = ksum + k_refs[ppb * mb + pp][...]
        blk = s * (PAGES_PER_STEP // ppb) + mb
        for t in range(n_new):
            prod = ksum * qb_ref[t]
            col = jnp.sum(jnp.sum(prod, axis=1, keepdims=True), axis=2, keepdims=True) * (1.0 / MOBA_BLOCK)
            gacc[t] = jnp.where(lane == blk, col, gacc[t])

    @pl.when(s == pl.num_programs(1) - 1)
    def _select():
        lanef = lane.astype(F32)
        out = jnp.zeros((ATT_HEADS, 1, LANES), F32)
        for t in range(n_new):
            cur = gacc[t]
            for j in range(MOBA_TOPK):
                m = jnp.max(cur, axis=2, keepdims=True)
                idx = jnp.min(jnp.where(cur == m, lanef, float(LANES)), axis=2, keepdims=True)
                out = jnp.where(lane == t * 4 + j, idx, out)
                cur = jnp.where(lanef == idx, -jnp.inf, cur)
        o_ref[...] = out


def page_gates(cache_kt, page_table, qb, n_new, layer):
    db, npages = page_table.shape
    assert npages % PAGES_PER_STEP == 0 and npages * PAGE // MOBA_BLOCK <= LANES and n_new <= 4

    def page_spec(u):
        return pl.BlockSpec((None, None, ATT_HEADS, ATT_DH, PAGE),
                            lambda b, s, pt: (layer, pt[b, s * PAGES_PER_STEP + u], 0, 0, 0))

    return pl.pallas_call(
        functools.partial(_page_gate_kernel, n_new=n_new),
        grid_spec=pltpu.PrefetchScalarGridSpec(
            num_scalar_prefetch=1,
            grid=(db, npages // PAGES_PER_STEP),
            in_specs=[page_spec(u) for u in range(PAGES_PER_STEP)]
            + [pl.BlockSpec((None, n_new, ATT_HEADS, ATT_DH, PAGE), lambda b, s, pt: (b, 0, 0, 0, 0))],
            out_specs=pl.BlockSpec((None, ATT_HEADS, 1, LANES), lambda b, s, pt: (b, 0, 0, 0)),
            scratch_shapes=[pltpu.VMEM((n_new, ATT_HEADS, 1, LANES), F32)],
        ),
        out_shape=jax.ShapeDtypeStruct((db, ATT_HEADS, 1, LANES), F32),
        compiler_params=_cparams(("arbitrary", "arbitrary")),
        name="moba_page_gates",
    )(page_table, *([cache_kt] * PAGES_PER_STEP), qb)


def _sample_attn_kernel(phys_ref, q_ref, kn_ref, vn_ref, ck_ref, cv_ref, o_ref, kbuf, vbuf, sems, *, n_new, layer):
    step = pl.program_id(0) * ATT_HEADS + pl.program_id(1)
    nsteps = pl.num_programs(0) * ATT_HEADS
    ppb = MOBA_BLOCK // PAGE
    per_tok = MOBA_TOPK * ppb
    per_step = n_new * per_tok

    def copies(st):
        hh = st % ATT_HEADS
        slot = st % 2
        out = []
        for u in range(per_step):
            ph = phys_ref[st * per_step + u]
            out.append(pltpu.make_async_copy(ck_ref.at[layer, ph, hh], kbuf.at[slot, u], sems.at[slot]))
            out.append(pltpu.make_async_copy(cv_ref.at[layer, ph, hh], vbuf.at[slot, u], sems.at[slot]))
        return out

    @pl.when(step == 0)
    def _first():
        for cp in copies(step):
            cp.start()

    @pl.when(step + 1 < nsteps)
    def _next():
        for cp in copies(step + 1):
            cp.start()

    for cp in copies(step):
        cp.wait()

    slot = step % 2
    q = q_ref[...]
    kn = _bf16_round(kn_ref[...])
    vn = _bf16_round(vn_ref[...])
    lane1 = lax.broadcasted_iota(jnp.int32, (1, LANES), 1)
    lane = lax.broadcasted_iota(jnp.int32, (ATT_DH, LANES), 1)
    out = jnp.zeros((ATT_DH, LANES), F32)
    for t in range(n_new):
        qc = q[:, t:t + 1]
        kk = _bf16_round(kbuf[slot, pl.ds(t * per_tok, per_tok)])
        vv = _bf16_round(vbuf[slot, pl.ds(t * per_tok, per_tok)])
        s = jnp.sum(kk * qc[None], axis=1, keepdims=True)
        so = jnp.sum(kn * qc, axis=0, keepdims=True)
        so = jnp.where(lane1 <= t, so, -jnp.inf)
        ms = jnp.max(jnp.max(s, axis=0, keepdims=True), axis=2, keepdims=True)
        m = jnp.maximum(ms[0], jnp.max(so, axis=1, keepdims=True))
        p = jnp.exp(s - m[None])
        po = jnp.exp(so - m)
        den = (jnp.sum(jnp.sum(p, axis=0, keepdims=True), axis=2, keepdims=True)[0]
               + jnp.sum(po, axis=1, keepdims=True))
        pv = jnp.sum(vv * _bf16_round(p / den[None]), axis=0)
        num = jnp.sum(pv, axis=1, keepdims=True) + jnp.sum(vn * _bf16_round(po / den), axis=1, keepdims=True)
        out = jnp.where(lane == t, num, out)
    o_ref[...] = out


def sample_attn(phys, q_t, kn_t, vn_t, cache_kt, cache_vt, n_new, layer):
    db = q_t.shape[0]
    hspec = pl.BlockSpec((None, None, ATT_DH, LANES), lambda b, h, ph: (b, h, 0, 0))
    pages = n_new * MOBA_TOPK * (MOBA_BLOCK // PAGE)
    return pl.pallas_call(
        functools.partial(_sample_attn_kernel, n_new=n_new, layer=layer),
        grid_spec=pltpu.PrefetchScalarGridSpec(
            num_scalar_prefetch=1,
            grid=(db, ATT_HEADS),
            in_specs=[hspec, hspec, hspec, pl.BlockSpec(memory_space=pl.ANY), pl.BlockSpec(memory_space=pl.ANY)],
            out_specs=hspec,
            scratch_shapes=[pltpu.VMEM((2, pages, ATT_DH, PAGE), F32), pltpu.VMEM((2, pages, ATT_DH, PAGE), F32),
                            pltpu.SemaphoreType.DMA((2,))],
        ),
        out_shape=jax.ShapeDtypeStruct((db, ATT_HEADS, ATT_DH, LANES), F32),
        compiler_params=_cparams(("arbitrary", "arbitrary")),
        name="moba_sample_attn",
    )(phys, q_t, kn_t, vn_t, cache_kt, cache_vt)


def _rot_tables(pos, theta, rot_dims):
    half = rot_dims // 2
    inv = jnp.power(theta, -jnp.arange(half, dtype=F32) / half)
    ang = pos.astype(F32)[:, None] * inv[None, :]
    return jnp.cos(ang), jnp.sin(ang)


def _positions(s_len, past_len, n_new, n_sample_rows):
    r = jnp.arange(n_sample_rows) % SP
    return jnp.concatenate([jnp.arange(s_len), past_len + jnp.minimum(r, n_new - 1)])


def _attn_rot_tables(pos):
    cos, sin = _rot_tables(pos, ROPE_THETA, ROT_DIMS)
    half = ROT_DIMS // 2
    n = pos.shape[0]
    one = jnp.ones((n, ATT_DH - ROT_DIMS), F32)
    zero8 = jnp.zeros((n, half), F32)
    zrest = jnp.zeros((n, ATT_DH - ROT_DIMS), F32)
    c64 = jnp.concatenate([cos, cos, one], axis=1)
    s1 = jnp.concatenate([-sin, zero8, zrest], axis=1)
    s2 = jnp.concatenate([zero8, sin, zrest], axis=1)
    rep = lambda a: jnp.concatenate([a, a], axis=1).reshape(n // TM, TM, LANES)
    return rep(c64), rep(s1), rep(s2)


def _ret_tables(l_real, l_pad):
    lg = jnp.log1p(-jnp.exp2(-5.0 - jnp.arange(RET_HEADS, dtype=F32)))
    idx = jnp.arange(l_real, dtype=F32)
    diff = idx[:, None] - idx[None, :]
    dmat = jnp.where(diff >= 0, jnp.exp(jnp.maximum(diff, 0.0)[None] * lg[:, None, None]), 0.0)
    q_decay = jnp.exp((idx[:, None] + 1.0) * lg[None, :]).T
    k_decay = jnp.exp((l_real - 1.0 - idx)[:, None] * lg[None, :]).T
    pad = l_pad - l_real
    dmat = jnp.pad(dmat, ((0, 0), (0, pad), (0, pad)))
    q_decay = jnp.pad(q_decay, ((0, 0), (0, pad)))[:, :, None]
    k_decay = jnp.pad(k_decay, ((0, 0), (0, pad)))[:, :, None]
    s_decay = jnp.broadcast_to(jnp.exp(l_real * lg)[:, None, None], (RET_HEADS, 1, RET_DV))
    return dmat, q_decay, k_decay, s_decay


def _mod_tables(m, nb, db):
    out = []
    for j in range(6):
        mj = m[:, j * D:(j + 1) * D]
        mp = jnp.broadcast_to(mj[:nb, None, :], (nb, TM, D))
        ms = jnp.repeat(mj[nb:nb + db], SP, axis=0).reshape(-1, TM, D)
        out.append(jnp.concatenate([mp, ms], axis=0))
    return out


def kernel(x_prompt, x_sample, c_prompt, c_sample, state_ret, cache_k, cache_v, page_table, ret_w_in, ret_gn_g, ret_w_out, att_w_qkv, att_w_out, ada_w, ada_b, ln_mix_g, ln_mix_b, ln_ffn_g, ln_ffn_b, router_w, router_b, exp_w_gu, exp_b_gu, exp_w_down, exp_b_down):
    nb, s_len, _ = x_prompt.shape
    db, n_new, _ = x_sample.shape
    past_len = page_table.shape[1] * PAGE
    assert past_len % MOBA_BLOCK == 0 and s_len % TM == 0 and (db * SP) % TM == 0 and n_new <= SP
    tp, ts = nb * s_len, db * SP
    npt, nst, tpb = tp // TM, ts // TM, s_len // TM
    nt = npt + nst
    geo = {
        "nt": nt,
        "mod_idx": lambda i: jnp.where(i < npt, i // tpb, nb + i - npt),
        "tab_idx": lambda i: jnp.where(i < npt, i % tpb, tpb + i - npt),
    }

    x = jnp.concatenate([x_prompt.reshape(tp, D),
                         jnp.pad(x_sample, ((0, 0), (0, SP - n_new), (0, 0))).reshape(ts, D)], axis=0)
    mods = ada_mods(jnp.concatenate([c_prompt, c_sample], axis=0), ada_w, ada_b)

    pos = _positions(s_len, past_len, n_new, ts)
    rcos, rsin = _rot_tables(pos, RET_THETA, RET_DK)
    rcos, rsin = rcos.reshape(-1, TM, LANES), rsin.reshape(-1, TM, LANES)
    atabs = _attn_rot_tables(pos)
    wr_pad = jnp.pad(router_w, ((0, 0), (0, 0), (0, LANES - N_EXPERTS))).astype(BF16)
    br_pad = jnp.pad(router_b, ((0, 0), (0, LANES - N_EXPERTS)), constant_values=NEG)
    cache_kt = cache_k.transpose(0, 1, 3, 4, 2)
    cache_vt = cache_v.transpose(0, 1, 3, 4, 2)

    st_p, st_s, kp_l, vp_l, ks_l, vs_l = [], [], [], [], [], []
    for i in range(DEPTH):
        j = i // 2
        m = _mod_tables(mods[i], nb, db)
        if i % 2 == 0:
            q, k, v, g = ret_proj(x, m[0], m[1], ret_w_in[j].astype(BF16), rcos, rsin, geo)
            yp, sp_state = ret_chunks(q, k, v, g, _ret_tables(RET_CHUNK, RET_CHUNK), ret_gn_g[j], None,
                                      nb, s_len // RET_CHUNK, RET_CHUNK, 0, BF16)
            ys, ss_state = ret_chunks(q[tp:].astype(F32), k[tp:], v[tp:].astype(F32), g[tp:], _ret_tables(n_new, SP),
                                      ret_gn_g[j], state_ret[j], db, 1, SP, 0, F32)
            y = jnp.concatenate([yp, ys.astype(BF16)], axis=0)
            w_out = ret_w_out[j].astype(BF16)
            st_p.append(sp_state)
            st_s.append(ss_state)
        else:
            q, kf, kb, vf, vb, kmean = attn_proj(x, m[0], m[1], att_w_qkv[j].astype(BF16), *atabs, geo)
            op = moba_attn(q, kb, vb, kmean[:npt].reshape(nb, tpb, D), nb, s_len)
            to_heads = lambda a: a[tp:].astype(F32).reshape(db, SP, ATT_HEADS, ATT_DH).transpose(0, 2, 3, 1)
            lane_pad = lambda a: jnp.pad(a, ((0, 0), (0, 0), (0, 0), (0, LANES - SP)))
            qh = to_heads(q)
            qb = jnp.broadcast_to(qh.transpose(0, 3, 1, 2)[:, :n_new, :, :, None], (db, n_new, ATT_HEADS, ATT_DH, PAGE))
            sel = page_gates(cache_kt, page_table, qb, n_new, j)
            sel = sel.reshape(db, ATT_HEADS, LANES)[:, :, :16].reshape(db, ATT_HEADS, 4, 4)
            sel = sel[:, :, :n_new, :MOBA_TOPK].astype(jnp.int32)
            ppb = MOBA_BLOCK // PAGE
            lpage = sel[..., None] * ppb + jnp.arange(ppb, dtype=jnp.int32)
            phys = page_table[jnp.arange(db)[:, None, None, None, None], lpage].reshape(-1).astype(jnp.int32)
            oh = sample_attn(phys, lane_pad(qh), lane_pad(to_heads(kf)), lane_pad(to_heads(vf)),
                             cache_kt, cache_vt, n_new, j)
            os_rows = oh[..., :SP].transpose(0, 3, 1, 2).reshape(ts, D).astype(BF16)
            y = jnp.concatenate([op, os_rows], axis=0)
            w_out = att_w_out[j].astype(BF16)
            shp_p = (nb, s_len, ATT_HEADS, ATT_DH)
            shp_s = (db, SP, ATT_HEADS, ATT_DH)
            kp_l.append(kf[:tp].reshape(shp_p))
            vp_l.append(vf[:tp].reshape(shp_p))
            ks_l.append(kf[tp:].reshape(shp_s)[:, :n_new])
            vs_l.append(vf[tp:].reshape(shp_s)[:, :n_new])
        x, t_rows, ti_f, tg = outproj_router(y, w_out, x, m[2], ln_mix_g[i], ln_mix_b[i], m[3], m[4],
                                             wr_pad[i], br_pad[i].reshape(1, LANES), geo)
        x = moe_ffn_norm(t_rows, ti_f, tg, x, m[5], ln_ffn_g[i], ln_ffn_b[i],
                         exp_w_gu, exp_b_gu, exp_w_down, exp_b_down, i, geo)

    y_prompt = x[:tp].reshape(nb, s_len, D)
    y_sample = x[tp:].reshape(db, SP, D)[:, :n_new]
    return (y_prompt, y_sample, jnp.stack(st_p), jnp.stack(st_s), jnp.stack(kp_l), jnp.stack(vp_l),
            jnp.stack(ks_l), jnp.stack(vs_l))
```

```python
import functools

import jax
import jax.numpy as jnp
from jax import lax
from jax.experimental import pallas as pl
from jax.experimental.pallas import tpu as pltpu

F32 = jnp.float32
BF16 = jnp.bfloat16

D = 1024
DEPTH = 2
RET_HEADS, RET_DK, RET_DV, RET_CHUNK, RET_THETA = 4, 256, 512, 128, 10000.0
RET_COLS = 2 * RET_HEADS * RET_DK + 2 * RET_HEADS * RET_DV
ATT_HEADS, ATT_DH, ROT_DIMS, ROPE_THETA = 16, 64, 16, 500000.0
MOBA_BLOCK, MOBA_TOPK, PAGE = 256, 3, 128
N_EXPERTS, TOP_K, D_FF = 32, 4, 1024
SWIGLU_LIMIT, SWIGLU_ALPHA = 7.0, 1.702
DN_ALPHA = (2 * DEPTH) ** 0.25
LN_EPS = 1e-5

TM = 256
SP = 8
TE = 256
PAGES_PER_STEP = 8
LANES = 128
VMEM_LIMIT = 56 * 1024 * 1024
NEG = -1e30


def _cparams(sem, **kw):
    return pltpu.CompilerParams(dimension_semantics=sem, vmem_limit_bytes=VMEM_LIMIT, **kw)


def _bf16_round(x):
    return x.astype(BF16).astype(F32)


def _layer_norm(r, g, b):
    mu = jnp.mean(r, axis=-1, keepdims=True)
    c = r - mu
    var = jnp.mean(c * c, axis=-1, keepdims=True)
    return c * lax.rsqrt(var + LN_EPS) * g + b


def _ada_kernel(c_ref, w_ref, b_ref, o_ref):
    c = c_ref[...]
    s = (c * jax.nn.sigmoid(c)).astype(BF16)
    o_ref[...] = jnp.dot(s, w_ref[...].astype(BF16), preferred_element_type=F32) + b_ref[...]


def ada_mods(c_all, ada_w, ada_b):
    nl, _, ncol = ada_w.shape
    nc = c_all.shape[0]
    tn = 1536
    return pl.pallas_call(
        _ada_kernel,
        grid=(nl, ncol // tn),
        in_specs=[pl.BlockSpec((nc, D), lambda l, j: (0, 0)),
                  pl.BlockSpec((None, D, tn), lambda l, j: (l, 0, j)),
                  pl.BlockSpec((None, 1, tn), lambda l, j: (l, 0, j))],
        out_specs=pl.BlockSpec((None, nc, tn), lambda l, j: (l, 0, j)),
        out_shape=jax.ShapeDtypeStruct((nl, nc, ncol), F32),
        compiler_params=_cparams(("arbitrary", "arbitrary")),
        name="ada_mods",
    )(c_all, ada_w, ada_b.reshape(nl, 1, ncol))


def _ret_proj_kernel(x_ref, sh_ref, sc_ref, w_ref, cos_ref, sin_ref, q_ref, k_ref, v_ref, g_ref):
    h = (x_ref[...] * (1.0 + sc_ref[...]) + sh_ref[...]).astype(BF16)
    cos = cos_ref[...]
    sin = sin_ref[...]
    nq = RET_HEADS * RET_DK
    nv = RET_HEADS * RET_DV
    half = RET_DK // 2
    zq = jnp.dot(h, w_ref[:, 0:nq], preferred_element_type=F32)
    zk = jnp.dot(h, w_ref[:, nq:2 * nq], preferred_element_type=F32)
    for hh in range(RET_HEADS):
        lo = hh * RET_DK
        x1, x2 = zq[:, lo:lo + half], zq[:, lo + half:lo + RET_DK]
        q_ref[:, lo:lo + half] = (x1 * cos - x2 * sin).astype(BF16)
        q_ref[:, lo + half:lo + RET_DK] = (x2 * cos + x1 * sin).astype(BF16)
        y1, y2 = zk[:, lo:lo + half], zk[:, lo + half:lo + RET_DK]
        k_ref[:, lo:lo + half] = (y1 * cos - y2 * sin) * (RET_DK ** -0.5)
        k_ref[:, lo + half:lo + RET_DK] = (y2 * cos + y1 * sin) * (RET_DK ** -0.5)
    v_ref[...] = jnp.dot(h, w_ref[:, 2 * nq:2 * nq + nv], preferred_element_type=F32).astype(BF16)
    g_ref[...] = jnp.dot(h, w_ref[:, 2 * nq + nv:], preferred_element_type=F32)


def ret_proj(x, sh, sc, w_bf, cos_t, sin_t, geo):
    nt, mod_idx, tab_idx = geo["nt"], geo["mod_idx"], geo["tab_idx"]
    t = x.shape[0]
    nq = RET_HEADS * RET_DK
    nv = RET_HEADS * RET_DV
    row = lambda w: pl.BlockSpec((TM, w), lambda i: (i, 0))
    mod = pl.BlockSpec((None, TM, D), lambda i: (mod_idx(i), 0, 0))
    tab = pl.BlockSpec((None, TM, LANES), lambda i: (tab_idx(i), 0, 0))
    return pl.pallas_call(
        _ret_proj_kernel,
        grid=(nt,),
        in_specs=[row(D), mod, mod, pl.BlockSpec((D, RET_COLS), lambda i: (0, 0)), tab, tab],
        out_specs=[row(nq), row(nq), row(nv), row(nv)],
        out_shape=[jax.ShapeDtypeStruct((t, nq), BF16), jax.ShapeDtypeStruct((t, nq), F32),
                   jax.ShapeDtypeStruct((t, nv), BF16), jax.ShapeDtypeStruct((t, nv), F32)],
        compiler_params=_cparams(("arbitrary",)),
        name="ret_proj",
    )(x, sh, sc, w_bf, cos_t, sin_t)


def _ret_chunk_kernel(*refs, has_state):
    if has_state:
        q_ref, k_ref, v_ref, g_ref, dm_ref, qd_ref, kd_ref, sd_ref, gn_ref, st0_ref, y_ref, st_ref = refs
    else:
        q_ref, k_ref, v_ref, g_ref, dm_ref, qd_ref, kd_ref, sd_ref, gn_ref, y_ref, st_ref = refs
    c = pl.program_id(2)

    @pl.when(c == 0)
    def _init():
        if has_state:
            st_ref[...] = st0_ref[...]
        else:
            st_ref[...] = jnp.zeros(st_ref.shape, F32)

    q = q_ref[...].astype(BF16)
    kf = k_ref[...]
    v = v_ref[...].astype(BF16)
    st = st_ref[...]
    scores = lax.dot_general(q, kf.astype(BF16), (((1,), (1,)), ((), ())), preferred_element_type=F32) * dm_ref[...]
    inner = jnp.dot(scores.astype(BF16), v, preferred_element_type=F32)
    cross = jnp.dot(q, st.astype(BF16), preferred_element_type=F32) * qd_ref[...]
    o = inner + cross
    kd = (kf * kd_ref[...]).astype(BF16)
    upd = lax.dot_general(kd, v, (((0,), (0,)), ((), ())), preferred_element_type=F32)
    st_ref[...] = sd_ref[...] * st + upd
    mu = jnp.mean(o, axis=-1, keepdims=True)
    cen = o - mu
    var = jnp.mean(cen * cen, axis=-1, keepdims=True)
    on = cen * lax.rsqrt(var + LN_EPS) * gn_ref[...]
    gg = g_ref[...]
    y_ref[...] = (gg * jax.nn.sigmoid(gg) * on).astype(y_ref.dtype)


def ret_chunks(q, k, v, g, tabs, gn_g, state0, nb, nc, blk_rows, row_blk0, y_dtype):
    dm, qd, kd, sd = tabs
    L = blk_rows
    has_state = state0 is not None
    rb = lambda b, h, c: (row_blk0 + b * nc + c, h)
    head3 = lambda shp: pl.BlockSpec((None,) + shp, lambda b, h, c: (h, 0, 0))
    in_specs = [pl.BlockSpec((L, RET_DK), rb), pl.BlockSpec((L, RET_DK), rb),
                pl.BlockSpec((L, RET_DV), rb), pl.BlockSpec((L, RET_DV), rb),
                head3((L, L)), head3((L, 1)), head3((L, 1)), head3((1, RET_DV)), head3((1, RET_DV))]
    args = [q, k, v, g, dm, qd, kd, sd, gn_g.reshape(RET_HEADS, 1, RET_DV)]
    st_spec = pl.BlockSpec((None, None, RET_DK, RET_DV), lambda b, h, c: (b, h, 0, 0))
    if has_state:
        in_specs.append(st_spec)
        args.append(state0)
    return pl.pallas_call(
        functools.partial(_ret_chunk_kernel, has_state=has_state),
        grid=(nb, RET_HEADS, nc),
        in_specs=in_specs,
        out_specs=[pl.BlockSpec((L, RET_DV), lambda b, h, c: (b * nc + c, h)), st_spec],
        out_shape=[jax.ShapeDtypeStruct((nb * nc * L, RET_HEADS * RET_DV), y_dtype),
                   jax.ShapeDtypeStruct((nb, RET_HEADS, RET_DK, RET_DV), F32)],
        compiler_params=_cparams(("arbitrary", "arbitrary", "arbitrary")),
        name="ret_chunks_state" if has_state else "ret_chunks",
    )(*args)


def _outproj_router_kernel(y_ref, w_ref, x_ref, gate_ref, lng_ref, lnb_ref, sh_ref, sc_ref, wr_ref, br_ref,
                           xn_ref, t_ref, ti_ref, tg_ref):
    f = jnp.dot(y_ref[...], w_ref[...], preferred_element_type=F32)
    r = DN_ALPHA * x_ref[...] + (1.0 + gate_ref[...]) * f
    xn = _layer_norm(r, lng_ref[...], lnb_ref[...])
    xn_ref[...] = xn
    t = xn * (1.0 + sc_ref[...]) + sh_ref[...]
    t_ref[...] = t
    logits = jnp.dot(t.astype(BF16), wr_ref[...], preferred_element_type=F32) + br_ref[...]
    lane = lax.broadcasted_iota(jnp.int32, logits.shape, 1).astype(F32)
    cur = logits
    vals, idxs = [], []
    for _ in range(TOP_K):
        m = jnp.max(cur, axis=1, keepdims=True)
        idx = jnp.min(jnp.where(cur == m, lane, float(LANES)), axis=1, keepdims=True)
        vals.append(m)
        idxs.append(idx)
        cur = jnp.where(lane == idx, -jnp.inf, cur)
    es = [jnp.exp(vv - vals[0]) for vv in vals]
    den = es[0] + es[1] + es[2] + es[3]
    ti = jnp.zeros(logits.shape, F32)
    tg = jnp.zeros(logits.shape, F32)
    for kk in range(TOP_K):
        ti = jnp.where(lane == float(kk), idxs[kk], ti)
        tg = jnp.where(lane == float(kk), es[kk] / den, tg)
    ti_ref[...] = ti
    tg_ref[...] = tg


def outproj_router(y, w_bf, x, gate, lng, lnb, sh, sc, wr_pad, br_pad, geo):
    nt, mod_idx = geo["nt"], geo["mod_idx"]
    t, kdim = y.shape
    row = lambda w: pl.BlockSpec((TM, w), lambda i: (i, 0))
    mod = pl.BlockSpec((None, TM, D), lambda i: (mod_idx(i), 0, 0))
    vec = lambda w: pl.BlockSpec((1, w), lambda i: (0, 0))
    return pl.pallas_call(
        _outproj_router_kernel,
        grid=(nt,),
        in_specs=[row(kdim), pl.BlockSpec((kdim, D), lambda i: (0, 0)), row(D), mod, vec(D), vec(D), mod, mod,
                  pl.BlockSpec((D, LANES), lambda i: (0, 0)), vec(LANES)],
        out_specs=[row(D), row(D), row(LANES), row(LANES)],
        out_shape=[jax.ShapeDtypeStruct((t, D), F32), jax.ShapeDtypeStruct((t, D), F32),
                   jax.ShapeDtypeStruct((t, LANES), F32), jax.ShapeDtypeStruct((t, LANES), F32)],
        compiler_params=_cparams(("arbitrary",)),
        name="outproj_router",
    )(y, w_bf, x, gate, lng.reshape(1, D), lnb.reshape(1, D), sh, sc, wr_pad, br_pad)


def _expert_kernel(be_ref, nvb_ref, idx_ref, nxt_ref, src_ref, wgu_ref, bgu_ref, wdn_ref, bdn_ref, o_ref,
                   xbuf, wgu_bf, wdn_bf, sems):
    i = pl.program_id(0)
    nvb = nvb_ref[0]

    def row_copy(blk, j, tok):
        return pltpu.make_async_copy(src_ref.at[pl.ds(tok, 1)], xbuf.at[blk % 2, pl.ds(j, 1)], sems.at[blk % 2])

    def issue(blk, ref):
        def body(j, c):
            row_copy(blk, j, ref[0, 0, j]).start()
            return c
        lax.fori_loop(0, TE, body, 0, unroll=8)

    @pl.when(i == 0)
    def _first():
        issue(i, idx_ref)

    @pl.when(i + 1 < nvb)
    def _next():
        issue(i + 1, nxt_ref)

    new_expert = (i == 0) | (be_ref[i] != be_ref[jnp.maximum(i - 1, 0)])

    @pl.when((i < nvb) & new_expert)
    def _cast_weights():
        wgu_bf[...] = wgu_ref[...].astype(BF16)
        wdn_bf[...] = wdn_ref[...].astype(BF16)

    @pl.when(i < nvb)
    def _compute():
        pltpu.make_async_copy(src_ref.at[pl.ds(0, TE)], xbuf.at[i % 2], sems.at[i % 2]).wait()
        x = xbuf[i % 2].astype(BF16)
        gu = jnp.dot(x, wgu_bf[...], preferred_element_type=F32) + bgu_ref[...]
        g = jnp.minimum(gu[:, :D_FF], SWIGLU_LIMIT)
        u = jnp.clip(gu[:, D_FF:], -SWIGLU_LIMIT, SWIGLU_LIMIT)
        a = (u + 1.0) * g * jax.nn.sigmoid(SWIGLU_ALPHA * g)
        o_ref[...] = jnp.dot(a.astype(BF16), wdn_bf[...], preferred_element_type=F32) + bdn_ref[...]

    @pl.when(i >= nvb)
    def _unused():
        o_ref[...] = jnp.zeros(o_ref.shape, F32)


def expert_blocks(t_rows, slot_tok, block_exp, nvb, wgu, bgu, wdn, bdn, layer, n_blocks):
    smem = lambda f: pl.BlockSpec((1, 1, TE), f, memory_space=pltpu.SMEM)
    wspec = lambda r, c: pl.BlockSpec((None, None, r, c), lambda i, be, nv: (layer, be[i], 0, 0))
    return pl.pallas_call(
        _expert_kernel,
        grid_spec=pltpu.PrefetchScalarGridSpec(
            num_scalar_prefetch=2,
            grid=(n_blocks,),
            in_specs=[smem(lambda i, be, nv: (i, 0, 0)),
                      smem(lambda i, be, nv: (jnp.minimum(i + 1, n_blocks - 1), 0, 0)),
                      pl.BlockSpec(memory_space=pl.ANY),
                      wspec(D, 2 * D_FF), wspec(1, 2 * D_FF), wspec(D_FF, D), wspec(1, D)],
            out_specs=pl.BlockSpec((TE, D), lambda i, be, nv: (i, 0)),
            scratch_shapes=[pltpu.VMEM((2, TE, D), F32), pltpu.VMEM((D, 2 * D_FF), BF16),
                            pltpu.VMEM((D_FF, D), BF16), pltpu.SemaphoreType.DMA((2,))],
        ),
        out_shape=jax.ShapeDtypeStruct((n_blocks * TE, D), F32),
        compiler_params=_cparams(("arbitrary",)),
        name="moe_expert_blocks",
    )(block_exp, nvb, slot_tok.reshape(n_blocks, 1, TE), slot_tok.reshape(n_blocks, 1, TE), t_rows,
      wgu, bgu.reshape(DEPTH, N_EXPERTS, 1, 2 * D_FF), wdn, bdn.reshape(DEPTH, N_EXPERTS, 1, D))


def _combine_norm_kernel(idx_ref, nxt_ref, src_ref, tg_ref, x_ref, gate_ref, lng_ref, lnb_ref, o_ref, buf, sems):
    i = pl.program_id(0)
    n = pl.num_programs(0)
    nrow = TOP_K * TM

    def row_copy(tile, j, row):
        return pltpu.make_async_copy(src_ref.at[pl.ds(row, 1)], buf.at[tile % 2, pl.ds(j, 1)], sems.at[tile % 2])

    def issue(tile, ref):
        def body(j, c):
            row_copy(tile, j, ref[0, 0, j]).start()
            return c
        lax.fori_loop(0, nrow, body, 0, unroll=8)

    @pl.when(i == 0)
    def _first():
        issue(i, idx_ref)

    @pl.when(i + 1 < n)
    def _next():
        issue(i + 1, nxt_ref)

    slot = i % 2
    pltpu.make_async_copy(src_ref.at[pl.ds(0, nrow)], buf.at[slot], sems.at[slot]).wait()
    tg = tg_ref[...]
    y = jnp.zeros((TM, D), F32)
    for kk in range(TOP_K):
        y = y + buf[slot, pl.ds(kk * TM, TM), :] * tg[:, kk:kk + 1]
    r = DN_ALPHA * x_ref[...] + (1.0 + gate_ref[...]) * y
    o_ref[...] = _layer_norm(r, lng_ref[...], lnb_ref[...])


def combine_norm(out_sorted, dest_tiles, tg, x, gate, lng, lnb, geo):
    nt, mod_idx = geo["nt"], geo["mod_idx"]
    t = x.shape[0]
    row = lambda w: pl.BlockSpec((TM, w), lambda i: (i, 0))
    vec = pl.BlockSpec((1, D), lambda i: (0, 0))
    smem = lambda f: pl.BlockSpec((1, 1, TOP_K * TM), f, memory_space=pltpu.SMEM)
    return pl.pallas_call(
        _combine_norm_kernel,
        grid=(nt,),
        in_specs=[smem(lambda i: (i, 0, 0)), smem(lambda i: (jnp.minimum(i + 1, nt - 1), 0, 0)),
                  pl.BlockSpec(memory_space=pl.ANY), row(LANES), row(D),
                  pl.BlockSpec((None, TM, D), lambda i: (mod_idx(i), 0, 0)), vec, vec],
        out_specs=row(D),
        out_shape=jax.ShapeDtypeStruct((t, D), F32),
        scratch_shapes=[pltpu.VMEM((2, TOP_K * TM, D), F32), pltpu.SemaphoreType.DMA((2,))],
        compiler_params=_cparams(("arbitrary",)),
        name="moe_combine_norm",
    )(dest_tiles, dest_tiles, out_sorted, tg, x, gate, lng.reshape(1, D), lnb.reshape(1, D))


def moe_ffn_norm(t_rows, ti_f, tg, x, gate, lng, lnb, wgu, bgu, wdn, bdn, layer, geo):
    t = t_rows.shape[0]
    nt = geo["nt"]
    ti = ti_f[:, :TOP_K].astype(jnp.int32)
    onehot = jnp.sum((ti[:, :, None] == jnp.arange(N_EXPERTS, dtype=jnp.int32)[None, None, :]).astype(jnp.int32), axis=1)
    csum = jnp.cumsum(onehot, axis=0)
    counts = csum[-1]
    rank = jnp.take_along_axis(csum - onehot, ti, axis=1)
    padded = (counts + TE - 1) // TE * TE
    pad_end = jnp.cumsum(padded)
    pad_start = pad_end - padded
    dest = (pad_start[ti] + rank).astype(jnp.int32)
    n_blocks = (t * TOP_K) // TE + N_EXPERTS
    tok = jnp.broadcast_to(jnp.arange(t, dtype=jnp.int32)[:, None], (t, TOP_K))
    slot_tok = jnp.zeros((n_blocks * TE,), jnp.int32).at[dest.reshape(-1)].set(tok.reshape(-1))
    nvb = (pad_end[-1] // TE).astype(jnp.int32).reshape(1)
    blk_ids = jnp.arange(n_blocks, dtype=jnp.int32)
    block_exp = jnp.minimum(jnp.searchsorted(pad_end, blk_ids * TE, side="right"), N_EXPERTS - 1).astype(jnp.int32)
    block_exp = jnp.where(blk_ids < nvb[0], block_exp, block_exp[nvb[0] - 1])
    out_sorted = expert_blocks(t_rows, slot_tok, block_exp, nvb, wgu, bgu, wdn, bdn, layer, n_blocks)
    dest_tiles = dest.reshape(nt, TM, TOP_K).transpose(0, 2, 1).reshape(nt, 1, TOP_K * TM)
    return combine_norm(out_sorted, dest_tiles, tg, x, gate, lng, lnb, geo)


def _attn_proj_kernel(x_ref, sh_ref, sc_ref, w_ref, c_ref, s1_ref, s2_ref,
                      q_ref, kf_ref, kb_ref, vf_ref, vb_ref, km_ref):
    h = (x_ref[...] * (1.0 + sc_ref[...]) + sh_ref[...]).astype(BF16)
    cc, s1, s2 = c_ref[...], s1_ref[...], s2_ref[...]

    def rot(z):
        return z * cc + pltpu.roll(z, LANES - ROT_DIMS // 2, 1) * s1 + pltpu.roll(z, ROT_DIMS // 2, 1) * s2

    zq = jnp.dot(h, w_ref[:, 0:D], preferred_element_type=F32)
    zk = jnp.dot(h, w_ref[:, D:2 * D], preferred_element_type=F32)
    zv = jnp.dot(h, w_ref[:, 2 * D:3 * D], preferred_element_type=F32)
    for c in range(D // LANES):
        lo = c * LANES
        q_ref[:, lo:lo + LANES] = (rot(zq[:, lo:lo + LANES]) * (ATT_DH ** -0.5)).astype(BF16)
        kr = rot(zk[:, lo:lo + LANES])
        kf_ref[:, lo:lo + LANES] = kr
        kb_ref[:, lo:lo + LANES] = kr.astype(BF16)
        km_ref[:, lo:lo + LANES] = jnp.mean(kr, axis=0, keepdims=True)
    vf_ref[...] = zv
    vb_ref[...] = zv.astype(BF16)


def attn_proj(x, sh, sc, w_bf, ctab, s1tab, s2tab, geo):
    nt, mod_idx, tab_idx = geo["nt"], geo["mod_idx"], geo["tab_idx"]
    t = x.shape[0]
    row = pl.BlockSpec((TM, D), lambda i: (i, 0))
    mod = pl.BlockSpec((None, TM, D), lambda i: (mod_idx(i), 0, 0))
    tab = pl.BlockSpec((None, TM, LANES), lambda i: (tab_idx(i), 0, 0))
    return pl.pallas_call(
        _attn_proj_kernel,
        grid=(nt,),
        in_specs=[row, mod, mod, pl.BlockSpec((D, 3 * D), lambda i: (0, 0)), tab, tab, tab],
        out_specs=[row, row, row, row, row, pl.BlockSpec((None, 1, D), lambda i: (i, 0, 0))],
        out_shape=[jax.ShapeDtypeStruct((t, D), BF16), jax.ShapeDtypeStruct((t, D), F32),
                   jax.ShapeDtypeStruct((t, D), BF16), jax.ShapeDtypeStruct((t, D), F32),
                   jax.ShapeDtypeStruct((t, D), BF16), jax.ShapeDtypeStruct((nt, 1, D), F32)],
        compiler_params=_cparams(("arbitrary",)),
        name="attn_proj",
    )(x, sh, sc, w_bf, ctab, s1tab, s2tab)


def _moba_attn_kernel(q_ref, k_ref, v_ref, km_ref, o_ref):
    qi = pl.program_id(2)
    q = q_ref[...]
    km = km_ref[...].astype(BF16)
    lane = lax.broadcasted_iota(jnp.int32, (MOBA_BLOCK, LANES), 1)
    lanef = lane.astype(F32)
    rows = lax.broadcasted_iota(jnp.int32, (MOBA_BLOCK, MOBA_BLOCK), 0)
    cols = lax.broadcasted_iota(jnp.int32, (MOBA_BLOCK, MOBA_BLOCK), 1)
    nt_dims = (((1,), (1,)), ((), ()))
    heads = []
    for hh in range(2):
        hm = (lane >= hh * ATT_DH) & (lane < (hh + 1) * ATT_DH)
        qh = jnp.where(hm, q, jnp.zeros_like(q))
        gates = lax.dot_general(qh, km, nt_dims, preferred_element_type=F32)
        cur = jnp.where(lane < qi, gates, -jnp.inf)
        selmask = jnp.zeros((MOBA_BLOCK, LANES), F32)
        for _ in range(MOBA_TOPK):
            m = jnp.max(cur, axis=1, keepdims=True)
            idx = jnp.min(jnp.where(cur == m, lanef, float(LANES)), axis=1, keepdims=True)
            selmask = jnp.where((lanef == idx) & (m > -jnp.inf), 1.0, selmask)
            cur = jnp.where(lanef == idx, -jnp.inf, cur)
        own0 = pl.multiple_of(qi * MOBA_BLOCK, MOBA_BLOCK)
        s = lax.dot_general(qh, k_ref[pl.ds(own0, MOBA_BLOCK), :], nt_dims, preferred_element_type=F32)
        s = jnp.where(cols <= rows, s, -jnp.inf)
        m0 = jnp.max(s, axis=1, keepdims=True)
        p = jnp.exp(s - m0)
        l0 = jnp.sum(p, axis=1, keepdims=True)
        acc0 = jnp.dot(p.astype(BF16), v_ref[pl.ds(own0, MOBA_BLOCK), :], preferred_element_type=F32)
        heads.append((qh, selmask, m0, l0, acc0))

    def body(nn, carry):
        r0 = pl.multiple_of(nn * MOBA_BLOCK, MOBA_BLOCK)
        kblk = k_ref[pl.ds(r0, MOBA_BLOCK), :]
        vblk = v_ref[pl.ds(r0, MOBA_BLOCK), :]
        out = []
        for hh in range(2):
            qh, selmask = heads[hh][0], heads[hh][1]
            m, l, acc = carry[hh]
            flag = jnp.sum(jnp.where(lane == nn, selmask, 0.0), axis=1, keepdims=True)
            s = lax.dot_general(qh, kblk, nt_dims, preferred_element_type=F32)
            s = jnp.where(flag > 0.5, s, -jnp.inf)
            mn = jnp.maximum(m, jnp.max(s, axis=1, keepdims=True))
            a = jnp.exp(m - mn)
            p = jnp.exp(s - mn)
            l = a * l + jnp.sum(p, axis=1, keepdims=True)
            acc = a * acc + jnp.dot(p.astype(BF16), vblk, preferred_element_type=F32)
            out.append((mn, l, acc))
        return tuple(out)

    fin = lax.fori_loop(0, qi, body, tuple((hd[2], hd[3], hd[4]) for hd in heads))
    o0 = fin[0][2] / fin[0][1]
    o1 = fin[1][2] / fin[1][1]
    o_ref[...] = jnp.where(lane < ATT_DH, o0, o1).astype(BF16)


def moba_attn(q, kb, vb, kmean, nb, s_len):
    nblk = s_len // MOBA_BLOCK
    assert nblk <= LANES
    hp = D // LANES
    kmean = jnp.pad(kmean, ((0, 0), (0, LANES - nblk), (0, 0)))
    return pl.pallas_call(
        _moba_attn_kernel,
        grid=(nb, hp, nblk),
        in_specs=[pl.BlockSpec((MOBA_BLOCK, LANES), lambda b, h, i: (b * nblk + i, h)),
                  pl.BlockSpec((s_len, LANES), lambda b, h, i: (b, h)),
                  pl.BlockSpec((s_len, LANES), lambda b, h, i: (b, h)),
                  pl.BlockSpec((None, LANES, LANES), lambda b, h, i: (b, 0, h))],
        out_specs=pl.BlockSpec((MOBA_BLOCK, LANES), lambda b, h, i: (b * nblk + i, h)),
        out_shape=jax.ShapeDtypeStruct((nb * s_len, D), BF16),
        compiler_params=_cparams(("arbitrary", "arbitrary", "arbitrary")),
        name="moba_attn",
    )(q, kb, vb, kmean)


def _page_gate_kernel(pt_ref, *refs, n_new):
    k_refs, qb_ref, o_ref, gacc = refs[:PAGES_PER_STEP], refs[PAGES_PER_STEP], refs[PAGES_PER_STEP + 1], refs[PAGES_PER_STEP + 2]
    s = pl.program_id(1)
    ppb = MOBA_BLOCK // PAGE
    lane = lax.broadcasted_iota(jnp.int32, (ATT_HEADS, 1, LANES), 2)

    @pl.when(s == 0)
    def _init():
        gacc[...] = jnp.full(gacc.shape, -jnp.inf, F32)

    for mb in range(PAGES_PER_STEP // ppb):
        ksum = k_refs[ppb * mb][...]
        for pp in range(1, ppb):
            ksum = ksum + k_refs[ppb * mb + pp][...]
        blk = s * (PAGES_PER_STEP // ppb) + mb
        for t in range(n_new):
            prod = ksum * qb_ref[t]
            col = jnp.sum(jnp.sum(prod, axis=1, keepdims=True), axis=2, keepdims=True) * (1.0 / MOBA_BLOCK)
            gacc[t] = jnp.where(lane == blk, col, gacc[t])

    @pl.when(s == pl.num_programs(1) - 1)
    def _select():
        lanef = lane.astype(F32)
        out = jnp.zeros((ATT_HEADS, 1, LANES), F32)
        for t in range(n_new):
            cur = gacc[t]
            for j in range(MOBA_TOPK):
                m = jnp.max(cur, axis=2, keepdims=True)
                idx = jnp.min(jnp.where(cur == m, lanef, float(LANES)), axis=2, keepdims=True)
                out = jnp.where(lane == t * 4 + j, idx, out)
                cur = jnp.where(lanef == idx, -jnp.inf, cur)
        o_ref[...] = out


def page_gates(cache_kt, page_table, qb, n_new, layer):
    db, npages = page_table.shape
    assert npages % PAGES_PER_STEP == 0 and npages * PAGE // MOBA_BLOCK <= LANES and n_new <= 4

    def page_spec(u):
        return pl.BlockSpec((None, None, ATT_HEADS, ATT_DH, PAGE),
                            lambda b, s, pt: (layer, pt[b, s * PAGES_PER_STEP + u], 0, 0, 0))

    return pl.pallas_call(
        functools.partial(_page_gate_kernel, n_new=n_new),
        grid_spec=pltpu.PrefetchScalarGridSpec(
            num_scalar_prefetch=1,
            grid=(db, npages // PAGES_PER_STEP),
            in_specs=[page_spec(u) for u in range(PAGES_PER_STEP)]
            + [pl.BlockSpec((None, n_new, ATT_HEADS, ATT_DH, PAGE), lambda b, s, pt: (b, 0, 0, 0, 0))],
            out_specs=pl.BlockSpec((None, ATT_HEADS, 1, LANES), lambda b, s, pt: (b, 0, 0, 0)),
            scratch_shapes=[pltpu.VMEM((n_new, ATT_HEADS, 1, LANES), F32)],
        ),
        out_shape=jax.ShapeDtypeStruct((db, ATT_HEADS, 1, LANES), F32),
        compiler_params=_cparams(("arbitrary", "arbitrary")),
        name="moba_page_gates",
    )(page_table, *([cache_kt] * PAGES_PER_STEP), qb)


def _sample_attn_kernel(phys_ref, q_ref, kn_ref, vn_ref, ck_ref, cv_ref, o_ref, kbuf, vbuf, sems, *, n_new, layer):
    step = pl.program_id(0) * ATT_HEADS + pl.program_id(1)
    nsteps = pl.num_programs(0) * ATT_HEADS
    ppb = MOBA_BLOCK // PAGE
    per_tok = MOBA_TOPK * ppb
    per_step = n_new * per_tok

    def copies(st):
        hh = st % ATT_HEADS
        slot = st % 2
        out = []
        for u in range(per_step):
            ph = phys_ref[st * per_step + u]
            out.append(pltpu.make_async_copy(ck_ref.at[layer, ph, hh], kbuf.at[slot, u], sems.at[slot]))
            out.append(pltpu.make_async_copy(cv_ref.at[layer, ph, hh], vbuf.at[slot, u], sems.at[slot]))
        return out

    @pl.when(step == 0)
    def _first():
        for cp in copies(step):
            cp.start()

    @pl.when(step + 1 < nsteps)
    def _next():
        for cp in copies(step + 1):
            cp.start()

    for cp in copies(step):
        cp.wait()

    slot = step % 2
    q = q_ref[...]
    kn = _bf16_round(kn_ref[...])
    vn = _bf16_round(vn_ref[...])
    lane1 = lax.broadcasted_iota(jnp.int32, (1, LANES), 1)
    lane = lax.broadcasted_iota(jnp.int32, (ATT_DH, LANES), 1)
    out = jnp.zeros((ATT_DH, LANES), F32)
    for t in range(n_new):
        qc = q[:, t:t + 1]
        kk = _bf16_round(kbuf[slot, pl.ds(t * per_tok, per_tok)])
        vv = _bf16_round(vbuf[slot, pl.ds(t * per_tok, per_tok)])
        s = jnp.sum(kk * qc[None], axis=1, keepdims=True)
        so = jnp.sum(kn * qc, axis=0, keepdims=True)
        so = jnp.where(lane1 <= t, so, -jnp.inf)
        ms = jnp.max(jnp.max(s, axis=0, keepdims=True), axis=2, keepdims=True)
        m = jnp.maximum(ms[0], jnp.max(so, axis=1, keepdims=True))
        p = jnp.exp(s - m[None])
        po = jnp.exp(so - m)
        den = (jnp.sum(jnp.sum(p, axis=0, keepdims=True), axis=2, keepdims=True)[0]
               + jnp.sum(po, axis=1, keepdims=True))
        pv = jnp.sum(vv * _bf16_round(p / den[None]), axis=0)
        num = jnp.sum(pv, axis=1, keepdims=True) + jnp.sum(vn * _bf16_round(po / den), axis=1, keepdims=True)
        out = jnp.where(lane == t, num, out)
    o_ref[...] = out


def sample_attn(phys, q_t, kn_t, vn_t, cache_kt, cache_vt, n_new, layer):
    db = q_t.shape[0]
    hspec = pl.BlockSpec((None, None, ATT_DH, LANES), lambda b, h, ph: (b, h, 0, 0))
    pages = n_new * MOBA_TOPK * (MOBA_BLOCK // PAGE)
    return pl.pallas_call(
        functools.partial(_sample_attn_kernel, n_new=n_new, layer=layer),
        grid_spec=pltpu.PrefetchScalarGridSpec(
            num_scalar_prefetch=1,
            grid=(db, ATT_HEADS),
            in_specs=[hspec, hspec, hspec, pl.BlockSpec(memory_space=pl.ANY), pl.BlockSpec(memory_space=pl.ANY)],
            out_specs=hspec,
            scratch_shapes=[pltpu.VMEM((2, pages, ATT_DH, PAGE), F32), pltpu.VMEM((2, pages, ATT_DH, PAGE), F32),
                            pltpu.SemaphoreType.DMA((2,))],
        ),
        out_shape=jax.ShapeDtypeStruct((db, ATT_HEADS, ATT_DH, LANES), F32),
        compiler_params=_cparams(("arbitrary", "arbitrary")),
        name="moba_sample_attn",
    )(phys, q_t, kn_t, vn_t, cache_kt, cache_vt)


def _rot_tables(pos, theta, rot_dims):
    half = rot_dims // 2
    inv = jnp.power(theta, -jnp.arange(half, dtype=F32) / half)
    ang = pos.astype(F32)[:, None] * inv[None, :]
    return jnp.cos(ang), jnp.sin(ang)


def _positions(s_len, past_len, n_new, n_sample_rows):
    r = jnp.arange(n_sample_rows) % SP
    return jnp.concatenate([jnp.arange(s_len), past_len + jnp.minimum(r, n_new - 1)])


def _attn_rot_tables(pos):
    cos, sin = _rot_tables(pos, ROPE_THETA, ROT_DIMS)
    half = ROT_DIMS // 2
    n = pos.shape[0]
    one = jnp.ones((n, ATT_DH - ROT_DIMS), F32)
    zero8 = jnp.zeros((n, half), F32)
    zrest = jnp.zeros((n, ATT_DH - ROT_DIMS), F32)
    c64 = jnp.concatenate([cos, cos, one], axis=1)
    s1 = jnp.concatenate([-sin, zero8, zrest], axis=1)
    s2 = jnp.concatenate([zero8, sin, zrest], axis=1)
    rep = lambda a: jnp.concatenate([a, a], axis=1).reshape(n // TM, TM, LANES)
    return rep(c64), rep(s1), rep(s2)


def _ret_tables(l_real, l_pad):
    lg = jnp.log1p(-jnp.exp2(-5.0 - jnp.arange(RET_HEADS, dtype=F32)))
    idx = jnp.arange(l_real, dtype=F32)
    diff = idx[:, None] - idx[None, :]
    dmat = jnp.where(diff >= 0, jnp.exp(jnp.maximum(diff, 0.0)[None] * lg[:, None, None]), 0.0)
    q_decay = jnp.exp((idx[:, None] + 1.0) * lg[None, :]).T
    k_decay = jnp.exp((l_real - 1.0 - idx)[:, None] * lg[None, :]).T
    pad = l_pad - l_real
    dmat = jnp.pad(dmat, ((0, 0), (0, pad), (0, pad)))
    q_decay = jnp.pad(q_decay, ((0, 0), (0, pad)))[:, :, None]
    k_decay = jnp.pad(k_decay, ((0, 0), (0, pad)))[:, :, None]
    s_decay = jnp.broadcast_to(jnp.exp(l_real * lg)[:, None, None], (RET_HEADS, 1, RET_DV))
    return dmat, q_decay, k_decay, s_decay


def _mod_tables(m, nb, db):
    out = []
    for j in range(6):
        mj = m[:, j * D:(j + 1) * D]
        mp = jnp.broadcast_to(mj[:nb, None, :], (nb, TM, D))
        ms = jnp.repeat(mj[nb:nb + db], SP, axis=0).reshape(-1, TM, D)
        out.append(jnp.concatenate([mp, ms], axis=0))
    return out


def kernel(x_prompt, x_sample, c_prompt, c_sample, state_ret, cache_k, cache_v, page_table, ret_w_in, ret_gn_g, ret_w_out, att_w_qkv, att_w_out, ada_w, ada_b, ln_mix_g, ln_mix_b, ln_ffn_g, ln_ffn_b, router_w, router_b, exp_w_gu, exp_b_gu, exp_w_down, exp_b_down):
    nb, s_len, _ = x_prompt.shape
    db, n_new, _ = x_sample.shape
    past_len = page_table.shape[1] * PAGE
    assert past_len % MOBA_BLOCK == 0 and s_len % TM == 0 and (db * SP) % TM == 0 and n_new <= SP
    tp, ts = nb * s_len, db * SP
    npt, nst, tpb = tp // TM, ts // TM, s_len // TM
    nt = npt + nst
    geo = {
        "nt": nt,
        "mod_idx": lambda i: jnp.where(i < npt, i // tpb, nb + i - npt),
        "tab_idx": lambda i: jnp.where(i < npt, i % tpb, tpb + i - npt),
    }

    x = jnp.concatenate([x_prompt.reshape(tp, D),
                         jnp.pad(x_sample, ((0, 0), (0, SP - n_new), (0, 0))).reshape(ts, D)], axis=0)
    mods = ada_mods(jnp.concatenate([c_prompt, c_sample], axis=0), ada_w, ada_b)

    pos = _positions(s_len, past_len, n_new, ts)
    rcos, rsin = _rot_tables(pos, RET_THETA, RET_DK)
    rcos, rsin = rcos.reshape(-1, TM, LANES), rsin.reshape(-1, TM, LANES)
    atabs = _attn_rot_tables(pos)
    wr_pad = jnp.pad(router_w, ((0, 0), (0, 0), (0, LANES - N_EXPERTS))).astype(BF16)
    br_pad = jnp.pad(router_b, ((0, 0), (0, LANES - N_EXPERTS)), constant_values=NEG)
    cache_kt = cache_k.transpose(0, 1, 3, 4, 2)
    cache_vt = cache_v.transpose(0, 1, 3, 4, 2)

    st_p, st_s, kp_l, vp_l, ks_l, vs_l = [], [], [], [], [], []
    for i in range(DEPTH):
        j = i // 2
        m = _mod_tables(mods[i], nb, db)
        if i % 2 == 0:
            q, k, v, g = ret_proj(x, m[0], m[1], ret_w_in[j].astype(BF16), rcos, rsin, geo)
            yp, sp_state = ret_chunks(q, k, v, g, _ret_tables(RET_CHUNK, RET_CHUNK), ret_gn_g[j], None,
                                      nb, s_len // RET_CHUNK, RET_CHUNK, 0, BF16)
            ys, ss_state = ret_chunks(q[tp:].astype(F32), k[tp:], v[tp:].astype(F32), g[tp:], _ret_tables(n_new, SP),
                                      ret_gn_g[j], state_ret[j], db, 1, SP, 0, F32)
            y = jnp.concatenate([yp, ys.astype(BF16)], axis=0)
            w_out = ret_w_out[j].astype(BF16)
            st_p.append(sp_state)
            st_s.append(ss_state)
        else:
            q, kf, kb, vf, vb, kmean = attn_proj(x, m[0], m[1], att_w_qkv[j].astype(BF16), *atabs, geo)
            op = moba_attn(q, kb, vb, kmean[:npt].reshape(nb, tpb, D), nb, s_len)
            to_heads = lambda a: a[tp:].astype(F32).reshape(db, SP, ATT_HEADS, ATT_DH).transpose(0, 2, 3, 1)
            lane_pad = lambda a: jnp.pad(a, ((0, 0), (0, 0), (0, 0), (0, LANES - SP)))
            qh = to_heads(q)
            qb = jnp.broadcast_to(qh.transpose(0, 3, 1, 2)[:, :n_new, :, :, None], (db, n_new, ATT_HEADS, ATT_DH, PAGE))
            sel = page_gates(cache_kt, page_table, qb, n_new, j)
            sel = sel.reshape(db, ATT_HEADS, LANES)[:, :, :16].reshape(db, ATT_HEADS, 4, 4)
            sel = sel[:, :, :n_new, :MOBA_TOPK].astype(jnp.int32)
            ppb = MOBA_BLOCK // PAGE
            lpage = sel[..., None] * ppb + jnp.arange(ppb, dtype=jnp.int32)
            phys = page_table[jnp.arange(db)[:, None, None, None, None], lpage].reshape(-1).astype(jnp.int32)
            oh = sample_attn(phys, lane_pad(qh), lane_pad(to_heads(kf)), lane_pad(to_heads(vf)),
                             cache_kt, cache_vt, n_new, j)
            os_rows = oh[..., :SP].transpose(0, 3, 1, 2).reshape(ts, D).astype(BF16)
            y = jnp.concatenate([op, os_rows], axis=0)
            w_out = att_w_out[j].astype(BF16)
            shp_p = (nb, s_len, ATT_HEADS, ATT_DH)
            shp_s = (db, SP, ATT_HEADS, ATT_DH)
            kp_l.append(kf[:tp].reshape(shp_p))
            vp_l.append(vf[:tp].reshape(shp_p))
            ks_l.append(kf[tp:].reshape(shp_s)[:, :n_new])
            vs_l.append(vf[tp:].reshape(shp_s)[:, :n_new])
        x, t_rows, ti_f, tg = outproj_router(y, w_out, x, m[2], ln_mix_g[i], ln_mix_b[i], m[3], m[4],
                                             wr_pad[i], br_pad[i].reshape(1, LANES), geo)
        x = moe_ffn_norm(t_rows, ti_f, tg, x, m[5], ln_ffn_g[i], ln_ffn_b[i],
                         exp_w_gu, exp_b_gu, exp_w_down, exp_b_down, i, geo)

    y_prompt = x[:tp].reshape(nb, s_len, D)
    y_sample = x[tp:].reshape(db, SP, D)[:, :n_new]
    return (y_prompt, y_sample, jnp.stack(st_p), jnp.stack(st_s), jnp.stack(kp_l), jnp.stack(vp_l),
            jnp.stack(ks_l), jnp.stack(vs_l))
```
